```python
import jax, jax.numpy as jnp
from jax import lax
import numpy as np

D_MODEL = 1024
BATCH = 8
SEQ = 4096
DEPTH = 4

D_MIX = D_MODEL
CONV_CH = D_MIX // 2
CONV_K = 31
SB_HEADS = 8
SB_HEAD_DIM = (D_MIX - CONV_CH) // SB_HEADS
SB_WIDTH = SB_HEADS * SB_HEAD_DIM
BLOCK_Q = 128
MEM_LEN = 256
MEM_HEADS = 4
MEM_HEAD_DIM = D_MODEL // MEM_HEADS
D_FF = ((8 * D_MODEL // 3 + 127) // 128) * 128
FFN_K = 3
IN_WIDTH = 2 * CONV_CH + 3 * SB_WIDTH
DEEPNORM_ALPHA = (2.0 * DEPTH) ** 0.25
DEEPNORM_BETA = (8.0 * DEPTH) ** -0.25
LN_EPS = 1e-5

kernel_name = "hybrid_conformer_stickbreak_deepnorm_trunk"


def layer_norm(x, g, b):
    xf = x.astype(jnp.float32)
    mu = jnp.mean(xf, axis=-1, keepdims=True)
    var = jnp.mean(jnp.square(xf - mu), axis=-1, keepdims=True)
    y = (xf - mu) * lax.rsqrt(var + LN_EPS)
    return (y * g.astype(jnp.float32) + b.astype(jnp.float32)).astype(x.dtype)


def causal_dwconv(x, w, b):
    k = w.shape[0]
    y = lax.conv_general_dilated(
        x, w[:, None, :].astype(x.dtype), window_strides=(1,), padding=[(k - 1, 0)],
        dimension_numbers=('NWC', 'WIO', 'NWC'), feature_group_count=x.shape[-1])
    return y + b


def stick_breaking_attention(q, k, v):
    bsz, seq, nh, dh = q.shape
    nb = seq // BLOCK_Q
    scale = dh ** -0.5
    qf = q.astype(jnp.float32).transpose(0, 2, 1, 3)
    kf = k.astype(jnp.float32).transpose(0, 2, 1, 3)
    vf = v.astype(jnp.float32).transpose(0, 2, 1, 3)
    q_blocks = qf.reshape(bsz, nh, nb, BLOCK_Q, dh).transpose(2, 0, 1, 3, 4)
    key_pos = jnp.arange(seq)

    def one_block(args):
        q_blk, blk = args
        q_pos = blk * BLOCK_Q + jnp.arange(BLOCK_Q)
        visible = key_pos[None, :] < q_pos[:, None]
        z = jnp.einsum('bhqd,bhkd->bhqk', q_blk, kf) * scale
        log_keep = jnp.where(visible, jax.nn.log_sigmoid(-z), 0.0)
        later = lax.cumsum(log_keep, axis=3, reverse=True) - log_keep
        a = jnp.where(visible, jnp.exp(jax.nn.log_sigmoid(z) + later), 0.0)
        return jnp.einsum('bhqk,bhkd->bhqd', a, vf)

    out = lax.map(one_block, (q_blocks, jnp.arange(nb)))
    out = out.transpose(1, 0, 3, 2, 4).reshape(bsz, seq, nh * dh)
    return out.astype(q.dtype)


def hybrid_mixer(h, w_in, conv_w, conv_b, conv_ln_g, conv_ln_b, w_out):
    bsz, seq, _ = h.shape
    proj = h @ w_in
    glu_a, glu_g, q, k, v = jnp.split(
        proj, [CONV_CH, 2 * CONV_CH, 2 * CONV_CH + SB_WIDTH, 2 * CONV_CH + 2 * SB_WIDTH], axis=-1)
    u = glu_a * jax.nn.sigmoid(glu_g)
    u = causal_dwconv(u, conv_w, conv_b)
    u = jax.nn.silu(layer_norm(u, conv_ln_g, conv_ln_b))
    heads = lambda t: t.reshape(bsz, seq, SB_HEADS, SB_HEAD_DIM)
    a = stick_breaking_attention(heads(q), heads(k), heads(v))
    return jnp.concatenate([u, a], axis=-1) @ w_out


def memory_cross_attention(h, mem, wq, wk, wv, wo):
    bsz, seq, _ = h.shape
    q = (h @ wq).reshape(bsz, seq, MEM_HEADS, MEM_HEAD_DIM).astype(jnp.float32)
    k = (mem @ wk).reshape(bsz, -1, MEM_HEADS, MEM_HEAD_DIM).astype(jnp.float32)
    v = (mem @ wv).reshape(bsz, -1, MEM_HEADS, MEM_HEAD_DIM).astype(jnp.float32)
    s = jnp.einsum('bqhd,bmhd->bhqm', q, k) * (MEM_HEAD_DIM ** -0.5)
    p = jax.nn.softmax(s, axis=-1)
    o = jnp.einsum('bhqm,bmhd->bqhd', p, v).reshape(bsz, seq, D_MODEL).astype(h.dtype)
    return o @ wo


def conv_gated_ffn(h, w_up, conv_w, conv_b, w_down):
    up = causal_dwconv(h @ w_up, conv_w, conv_b)
    val, gate = jnp.split(up, 2, axis=-1)
    return (jax.nn.silu(gate) * val) @ w_down


def setup_inputs(seed: int = 0) -> dict:
    key = jax.random.key(seed)
    ks = jax.random.split(key, 32)
    nrm = lambda k, shape, s: jax.random.normal(k, shape, jnp.float32) * s
    gain = lambda k, n: 1.0 + 0.05 * jax.random.normal(k, (DEPTH, n), jnp.float32)
    bias = lambda k, n: 0.01 * jax.random.normal(k, (DEPTH, n), jnp.float32)
    col = jnp.arange(IN_WIDTH)
    v_scale = jnp.where(col >= 2 * CONV_CH + 2 * SB_WIDTH, DEEPNORM_BETA, 1.0).astype(jnp.float32)
    w_in = nrm(ks[2], (DEPTH, D_MODEL, IN_WIDTH), D_MODEL ** -0.5) * v_scale
    return {
        "x": nrm(ks[0], (BATCH, SEQ, D_MODEL), 1.0),
        "mem": nrm(ks[1], (BATCH, MEM_LEN, D_MODEL), 1.0),
        "w_in": w_in,
        "conv_w": nrm(ks[3], (DEPTH, CONV_K, CONV_CH), CONV_K ** -0.5),
        "conv_b": bias(ks[4], CONV_CH),
        "conv_ln_g": gain(ks[5], CONV_CH),
        "conv_ln_b": bias(ks[6], CONV_CH),
        "w_out": nrm(ks[7], (DEPTH, D_MIX, D_MODEL), D_MIX ** -0.5 * DEEPNORM_BETA),
        "ln1_g": gain(ks[8], D_MODEL),
        "ln1_b": bias(ks[9], D_MODEL),
        "mem_wq": nrm(ks[10], (DEPTH, D_MODEL, D_MODEL), D_MODEL ** -0.5),
        "mem_wk": nrm(ks[11], (DEPTH, D_MODEL, D_MODEL), D_MODEL ** -0.5),
        "mem_wv": nrm(ks[12], (DEPTH, D_MODEL, D_MODEL), D_MODEL ** -0.5 * DEEPNORM_BETA),
        "mem_wo": nrm(ks[13], (DEPTH, D_MODEL, D_MODEL), D_MODEL ** -0.5 * DEEPNORM_BETA),
        "ln2_g": gain(ks[14], D_MODEL),
        "ln2_b": bias(ks[15], D_MODEL),
        "ffn_up": nrm(ks[16], (DEPTH, D_MODEL, 2 * D_FF), D_MODEL ** -0.5),
        "ffn_conv_w": nrm(ks[17], (DEPTH, FFN_K, 2 * D_FF), FFN_K ** -0.5),
        "ffn_conv_b": bias(ks[18], 2 * D_FF),
        "ffn_down": nrm(ks[19], (DEPTH, D_FF, D_MODEL), D_FF ** -0.5 * DEEPNORM_BETA),
        "ln3_g": gain(ks[20], D_MODEL),
        "ln3_b": bias(ks[21], D_MODEL),
    }


def reference(x, mem, w_in, conv_w, conv_b, conv_ln_g, conv_ln_b, w_out, ln1_g, ln1_b,
              mem_wq, mem_wk, mem_wv, mem_wo, ln2_g, ln2_b,
              ffn_up, ffn_conv_w, ffn_conv_b, ffn_down, ln3_g, ln3_b):
    for l in range(DEPTH):
        mix = hybrid_mixer(x, w_in[l], conv_w[l], conv_b[l], conv_ln_g[l], conv_ln_b[l], w_out[l])
        x = layer_norm(DEEPNORM_ALPHA * x + mix, ln1_g[l], ln1_b[l])
        cross = memory_cross_attention(x, mem, mem_wq[l], mem_wk[l], mem_wv[l], mem_wo[l])
        x = layer_norm(DEEPNORM_ALPHA * x + cross, ln2_g[l], ln2_b[l])
        ffn = conv_gated_ffn(x, ffn_up[l], ffn_conv_w[l], ffn_conv_b[l], ffn_down[l])
        x = layer_norm(DEEPNORM_ALPHA * x + ffn, ln3_g[l], ln3_b[l])
    return x
```

```python
import functools

import jax
import jax.numpy as jnp
from jax import lax
from jax.experimental import pallas as pl
from jax.experimental.pallas import tpu as pltpu

F32 = jnp.float32
BF16 = jnp.bfloat16

CONV_K = 31
SB_HEADS = 8
SB_HEAD_DIM = 64
MEM_HEADS = 4
FFN_K = 3
LN_EPS = 1e-5

LANES = 128
SUBLANES = 8
VMEM_LIMIT = 56 * 1024 * 1024

TM = 512
TQ = 256
TK = 128
FC = 256
CONV_HALO = 32
FFN_HALO = 8
LOG_STICK_FLOOR = -100.0


def _dot(a, b):
    return jnp.dot(a, b, preferred_element_type=F32)


def _dot_nt(a, b):
    return lax.dot_general(a, b, (((1,), (1,)), ((), ())), preferred_element_type=F32)


def _layer_norm(h, g, b):
    mu = jnp.mean(h, axis=-1, keepdims=True)
    d = h - mu
    var = jnp.mean(d * d, axis=-1, keepdims=True)
    return d * lax.rsqrt(var + LN_EPS) * g + b


def _params(n_axes):
    return pltpu.CompilerParams(
        dimension_semantics=("arbitrary",) * n_axes, vmem_limit_bytes=VMEM_LIMIT)


def _const_spec(shape):
    nd = len(shape)
    return pl.BlockSpec(shape, lambda *_: (0,) * nd, pipeline_mode=pl.Buffered(1))


def _memkv_kernel(mem_ref, wk_ref, wv_ref, k_ref, v_ref):
    m = mem_ref[...]
    k_ref[0] = _dot(m, wk_ref[0]).astype(BF16)
    v_ref[0] = _dot(m, wv_ref[0]).astype(BF16)


def _memkv(mem2, wk, wv, mem_len):
    depth, d, _ = wk.shape
    rows = mem2.shape[0]
    nb = rows // mem_len
    return pl.pallas_call(
        _memkv_kernel,
        grid=(depth, nb),
        in_specs=[
            pl.BlockSpec((mem_len, d), lambda l, b: (b, 0)),
            pl.BlockSpec((1, d, d), lambda l, b: (l, 0, 0)),
            pl.BlockSpec((1, d, d), lambda l, b: (l, 0, 0)),
        ],
        out_specs=[
            pl.BlockSpec((1, mem_len, d), lambda l, b: (l, b, 0)),
            pl.BlockSpec((1, mem_len, d), lambda l, b: (l, b, 0)),
        ],
        out_shape=[jax.ShapeDtypeStruct((depth, rows, d), BF16)] * 2,
        compiler_params=_params(2),
        name="mem_kv",
    )(mem2, wk, wv)


def _inproj_kernel(x_ref, w_ref, u_ref, qkv_ref, *, cch, sbw):
    xb = x_ref[...].astype(BF16)
    ga = _dot(xb, w_ref[:, 0:cch])
    gg = _dot(xb, w_ref[:, cch:2 * cch])
    u_ref[...] = ga * jax.nn.sigmoid(gg)
    q = _dot(xb, w_ref[:, 2 * cch:2 * cch + sbw]) * (SB_HEAD_DIM ** -0.5)
    qkv_ref[:, 0:sbw] = q.astype(BF16)
    qkv_ref[:, sbw:3 * sbw] = _dot(xb, w_ref[:, 2 * cch + sbw:]).astype(BF16)


def _inproj(x2, w_in, cch, sbw):
    n, d = x2.shape
    return pl.pallas_call(
        functools.partial(_inproj_kernel, cch=cch, sbw=sbw),
        grid=(n // TM,),
        in_specs=[
            pl.BlockSpec((TM, d), lambda i: (i, 0)),
            _const_spec(w_in.shape),
        ],
        out_specs=[
            pl.BlockSpec((TM, cch), lambda i: (i, 0)),
            pl.BlockSpec((TM, 3 * sbw), lambda i: (i, 0)),
        ],
        out_shape=[
            jax.ShapeDtypeStruct((n, cch), F32),
            jax.ShapeDtypeStruct((n, 3 * sbw), BF16),
        ],
        compiler_params=_params(1),
        name="in_proj",
    )(x2, w_in)


def _sb_attn_kernel(q_ref, k_ref, v_ref, tri_ref, o_ref, acc_ref, stick_ref):
    q0 = pl.program_id(2) * TQ
    lane = lax.broadcasted_iota(jnp.int32, (TQ, LANES), 1)
    qv = q_ref[...]
    zero = jnp.zeros_like(qv)
    q_heads = (jnp.where(lane < SB_HEAD_DIM, qv, zero), jnp.where(lane >= SB_HEAD_DIM, qv, zero))
    tri = tri_ref[...]
    acc_ref[...] = jnp.zeros_like(acc_ref)
    stick_ref[...] = jnp.zeros_like(stick_ref)
    n_blocks = (q0 + TQ) // TK
    q_pos = q0 + lax.broadcasted_iota(jnp.int32, (TQ, TK), 0)
    k_off = lax.broadcasted_iota(jnp.int32, (TQ, TK), 1)

    def cond(carry):
        j, live = carry
        return jnp.logical_and(j < n_blocks, live)

    def body(carry):
        j, _ = carry
        ks = pl.multiple_of(q0 + TQ - (j + 1) * TK, TK)
        kb = k_ref[pl.ds(ks, TK), :]
        vb = v_ref[pl.ds(ks, TK), :]
        visible = (ks + k_off) < q_pos
        top = jnp.float32(-jnp.inf)
        for h in range(2):
            z = _dot_nt(q_heads[h], kb)
            log_keep = -(jnp.maximum(z, 0.0) + jnp.log1p(jnp.exp(-jnp.abs(z))))
            log_keep = jnp.where(visible, log_keep, 0.0)
            hi = log_keep.astype(BF16)
            lo = (log_keep - hi.astype(F32)).astype(BF16)
            sums = _dot(hi, tri) + _dot(lo, tri)
            later = sums[:, :TK]
            total = sums[:, TK:]
            stick = stick_ref[h]
            a = jnp.where(visible, jnp.exp(z + log_keep + later + stick), 0.0)
            acc_ref[h] += _dot(a.astype(BF16), vb)
            stick = stick + total
            stick_ref[h] = stick
            top = jnp.maximum(top, jnp.max(stick))
        return j + 1, top > LOG_STICK_FLOOR

    lax.while_loop(cond, body, (jnp.int32(0), jnp.bool_(True)))
    o_ref[...] = jnp.where(lane < SB_HEAD_DIM, acc_ref[0], acc_ref[1]).astype(BF16)


def _sb_attention(qkv, bsz, seq, sbw):
    n = qkv.shape[0]
    pairs = sbw // LANES
    qblocks = seq // TQ
    r = lax.broadcasted_iota(jnp.int32, (TK, 2 * TK), 0)
    c = lax.broadcasted_iota(jnp.int32, (TK, 2 * TK), 1)
    tri = jnp.where(jnp.logical_or(c >= TK, r > c), 1.0, 0.0).astype(BF16)
    return pl.pallas_call(
        _sb_attn_kernel,
        grid=(bsz, pairs, qblocks),
        in_specs=[
            pl.BlockSpec((TQ, LANES), lambda b, p, i: (b * qblocks + i, p)),
            pl.BlockSpec((seq, LANES), lambda b, p, i: (b, pairs + p)),
            pl.BlockSpec((seq, LANES), lambda b, p, i: (b, 2 * pairs + p)),
            pl.BlockSpec((TK, 2 * TK), lambda b, p, i: (0, 0)),
        ],
        out_specs=pl.BlockSpec((TQ, LANES), lambda b, p, i: (b * qblocks + i, p)),
        out_shape=jax.ShapeDtypeStruct((n, sbw), BF16),
        scratch_shapes=[
            pltpu.VMEM((2, TQ, LANES), F32),
            pltpu.VMEM((2, TQ, LANES), F32),
        ],
        compiler_params=_params(3),
        name="sb_attn",
    )(qkv, qkv, qkv, tri)


def _mix_out_kernel(u_ref, uh_ref, a_ref, x_ref, cw_ref, cb_ref, cg_ref, cbeta_ref,
                    wo_ref, g_ref, b_ref, o_ref, ext_ref, *, tiles_per_seq, alpha, cch):
    at_start = (pl.program_id(0) % tiles_per_seq) == 0
    ext_ref[0:CONV_HALO] = jnp.where(at_start, 0.0, uh_ref[...])
    ext_ref[CONV_HALO:CONV_HALO + TM] = u_ref[...]
    y = jnp.zeros((TM, cch), F32) + cb_ref[...]
    for k in range(CONV_K):
        off = CONV_HALO - (CONV_K - 1) + k
        y = y + cw_ref[k:k + 1, :] * ext_ref[off:off + TM, :]
    yn = _layer_norm(y, cg_ref[...], cbeta_ref[...])
    act = yn * jax.nn.sigmoid(yn)
    mix = _dot(act.astype(BF16), wo_ref[0:cch, :]) + _dot(a_ref[...], wo_ref[cch:, :])
    o_ref[...] = _layer_norm(alpha * x_ref[...] + mix, g_ref[...], b_ref[...])


def _mix_out(u, a, x2, cw, cb, cg, cbeta, w_out, g, b, seq, alpha):
    n, d = x2.shape
    cch = u.shape[1]
    tiles_per_seq = seq // TM
    halo_blocks = TM // CONV_HALO
    row = lambda v: v.reshape(1, -1)
    return pl.pallas_call(
        functools.partial(_mix_out_kernel, tiles_per_seq=tiles_per_seq, alpha=alpha, cch=cch),
        grid=(n // TM,),
        in_specs=[
            pl.BlockSpec((TM, cch), lambda i: (i, 0)),
            pl.BlockSpec((CONV_HALO, cch), lambda i: (jnp.maximum(i * halo_blocks - 1, 0), 0)),
            pl.BlockSpec((TM, a.shape[1]), lambda i: (i, 0)),
            pl.BlockSpec((TM, d), lambda i: (i, 0)),
            _const_spec(cw.shape),
            _const_spec((1, cch)), _const_spec((1, cch)), _const_spec((1, cch)),
            _const_spec(w_out.shape),
            _const_spec((1, d)), _const_spec((1, d)),
        ],
        out_specs=pl.BlockSpec((TM, d), lambda i: (i, 0)),
        out_shape=jax.ShapeDtypeStruct((n, d), F32),
        scratch_shapes=[pltpu.VMEM((CONV_HALO + TM, cch), F32)],
        compiler_params=_params(1),
        name="mix_out",
    )(u, u, a, x2, cw, row(cb), row(cg), row(cbeta), w_out, row(g), row(b))


def _cross_kernel(x_ref, wq_ref, k_ref, v_ref, wo_ref, g_ref, b_ref, o_ref, *, alpha, dh):
    x = x_ref[...]
    q = (_dot(x.astype(BF16), wq_ref[...]) * (dh ** -0.5)).astype(BF16)
    outs = []
    for h in range(MEM_HEADS):
        sl = slice(h * dh, (h + 1) * dh)
        s = _dot_nt(q[:, sl], k_ref[:, sl])
        e = jnp.exp(s - jnp.max(s, axis=-1, keepdims=True))
        p = e / jnp.sum(e, axis=-1, keepdims=True)
        outs.append(_dot(p.astype(BF16), v_ref[:, sl]).astype(BF16))
    cross = _dot(jnp.concatenate(outs, axis=-1), wo_ref[...])
    o_ref[...] = _layer_norm(alpha * x + cross, g_ref[...], b_ref[...])


def _cross(x2, wq, k, v, wo, g, b, seq, mem_len, alpha):
    n, d = x2.shape
    tiles_per_seq = seq // TM
    row = lambda t: t.reshape(1, -1)
    return pl.pallas_call(
        functools.partial(_cross_kernel, alpha=alpha, dh=d // MEM_HEADS),
        grid=(n // TM,),
        in_specs=[
            pl.BlockSpec((TM, d), lambda i: (i, 0)),
            _const_spec(wq.shape),
            pl.BlockSpec((mem_len, d), lambda i: (i // tiles_per_seq, 0)),
            pl.BlockSpec((mem_len, d), lambda i: (i // tiles_per_seq, 0)),
            _const_spec(wo.shape),
            _const_spec((1, d)), _const_spec((1, d)),
        ],
        out_specs=pl.BlockSpec((TM, d), lambda i: (i, 0)),
        out_shape=jax.ShapeDtypeStruct((n, d), F32),
        compiler_params=_params(1),
        name="cross_attn",
    )(x2, wq, k, v, wo, row(g), row(b))


def _ffn_kernel(x_ref, xh_ref, wup_ref, cw_ref, cb_ref, wdn_ref, g_ref, b_ref, o_ref,
                xe_ref, val_ref, gate_ref, acc_ref, *, tiles_per_seq, alpha, n_chunks):
    at_start = (pl.program_id(0) % tiles_per_seq) == 0
    x = x_ref[...]
    xe_ref[0:FFN_HALO] = jnp.where(at_start, 0.0, xh_ref[...]).astype(BF16)
    xe_ref[FFN_HALO:FFN_HALO + TM] = x.astype(BF16)
    acc_ref[...] = jnp.zeros_like(acc_ref)

    def conv(e_ref, w, bias):
        base = FFN_HALO - (FFN_K - 1)
        y = bias
        for k in range(FFN_K):
            y = y + w[k:k + 1, :] * e_ref[base + k:base + k + TM, :]
        return y

    def chunk(c, carry):
        xe = xe_ref[...]
        val_ref[...] = _dot(xe, wup_ref[c])
        gate_ref[...] = _dot(xe, wup_ref[n_chunks + c])
        val = conv(val_ref, cw_ref[c], cb_ref[c])
        gate = conv(gate_ref, cw_ref[n_chunks + c], cb_ref[n_chunks + c])
        hid = (gate * jax.nn.sigmoid(gate)) * val
        acc_ref[...] += _dot(hid.astype(BF16), wdn_ref[c])
        return carry

    lax.fori_loop(0, n_chunks, chunk, 0)
    o_ref[...] = _layer_norm(alpha * x + acc_ref[...], g_ref[...], b_ref[...])


def _ffn(x2, wup, cw, cb, wdn, g, b, seq, alpha):
    n, d = x2.shape
    n_chunks = wdn.shape[0]
    tiles_per_seq = seq // TM
    halo_blocks = TM // FFN_HALO
    row = lambda t: t.reshape(1, -1)
    return pl.pallas_call(
        functools.partial(_ffn_kernel, tiles_per_seq=tiles_per_seq, alpha=alpha, n_chunks=n_chunks),
        grid=(n // TM,),
        in_specs=[
            pl.BlockSpec((TM, d), lambda i: (i, 0)),
            pl.BlockSpec((FFN_HALO, d), lambda i: (jnp.maximum(i * halo_blocks - 1, 0), 0)),
            _const_spec(wup.shape),
            _const_spec(cw.shape),
            _const_spec(cb.shape),
            _const_spec(wdn.shape),
            _const_spec((1, d)), _const_spec((1, d)),
        ],
        out_specs=pl.BlockSpec((TM, d), lambda i: (i, 0)),
        out_shape=jax.ShapeDtypeStruct((n, d), F32),
        scratch_shapes=[
            pltpu.VMEM((FFN_HALO + TM, d), BF16),
            pltpu.VMEM((FFN_HALO + TM, FC), F32),
            pltpu.VMEM((FFN_HALO + TM, FC), F32),
            pltpu.VMEM((TM, d), F32),
        ],
        compiler_params=_params(1),
        name="conv_ffn",
    )(x2, x2, wup, cw, cb, wdn, row(g), row(b))


def _chunk_cols(t, n_chunks):
    lead = t.shape[:-2]
    r = t.shape[-2]
    t = t.reshape(*lead, r, n_chunks, FC)
    return jnp.moveaxis(t, -2, -3)


def kernel(x, mem, w_in, conv_w, conv_b, conv_ln_g, conv_ln_b, w_out, ln1_g, ln1_b,
           mem_wq, mem_wk, mem_wv, mem_wo, ln2_g, ln2_b,
           ffn_up, ffn_conv_w, ffn_conv_b, ffn_down, ln3_g, ln3_b):
    bsz, seq, d = x.shape
    depth = w_in.shape[0]
    mem_len = mem.shape[1]
    cch = conv_w.shape[2]
    sbw = SB_HEADS * SB_HEAD_DIM
    d_ff = ffn_down.shape[1]
    n_chunks = d_ff // FC
    alpha = (2.0 * depth) ** 0.25
    assert seq % TM == 0 and seq % TQ == 0 and d_ff % FC == 0
    assert w_in.shape[2] == 2 * cch + 3 * sbw and sbw % LANES == 0

    w_in_b = w_in.astype(BF16)
    w_out_b = w_out.astype(BF16)
    wq_b = mem_wq.astype(BF16)
    wo_b = mem_wo.astype(BF16)
    wup_b = _chunk_cols(ffn_up.astype(BF16), 2 * n_chunks)
    fcw = _chunk_cols(ffn_conv_w, 2 * n_chunks)
    fcb = _chunk_cols(ffn_conv_b[:, None, :], 2 * n_chunks)
    wdn_b = ffn_down.astype(BF16).reshape(depth, n_chunks, FC, d)

    mem_k, mem_v = _memkv(mem.reshape(bsz * mem_len, d).astype(BF16),
                          mem_wk.astype(BF16), mem_wv.astype(BF16), mem_len)

    h = x.reshape(bsz * seq, d)
    for l in range(depth):
        u, qkv = _inproj(h, w_in_b[l], cch, sbw)
        a = _sb_attention(qkv, bsz, seq, sbw)
        h = _mix_out(u, a, h, conv_w[l], conv_b[l], conv_ln_g[l], conv_ln_b[l],
                     w_out_b[l], ln1_g[l], ln1_b[l], seq, alpha)
        h = _cross(h, wq_b[l], mem_k[l], mem_v[l], wo_b[l], ln2_g[l], ln2_b[l],
                   seq, mem_len, alpha)
        h = _ffn(h, wup_b[l], fcw[l], fcb[l], wdn_b[l], ln3_g[l], ln3_b[l], seq, alpha)
    return h.reshape(bsz, seq, d)
```

```python
import functools

import jax
import jax.numpy as jnp
from jax import lax
from jax.experimental import pallas as pl
from jax.experimental.pallas import tpu as pltpu

F32 = jnp.float32
BF16 = jnp.bfloat16

CONV_K = 31
SB_HEADS = 8
SB_HEAD_DIM = 64
MEM_HEADS = 4
FFN_K = 3
LN_EPS = 1e-5

LANES = 128
SUBLANES = 8
VMEM_LIMIT = 56 * 1024 * 1024

TM = 512
STEP_Q = 128
SQ = 64
WIN = 256
TK = 128
FC = 256
CONV_HALO = 32
FFN_HALO = 8
LOG_STICK_FLOOR = -100.0


def _dot(a, b):
    return jnp.dot(a, b, preferred_element_type=F32)


def _dot_nt(a, b):
    return lax.dot_general(a, b, (((1,), (1,)), ((), ())), preferred_element_type=F32)


def _layer_norm(h, g, b):
    mu = jnp.mean(h, axis=-1, keepdims=True)
    d = h - mu
    var = jnp.mean(d * d, axis=-1, keepdims=True)
    return d * lax.rsqrt(var + LN_EPS) * g + b


def _params(n_axes):
    return pltpu.CompilerParams(
        dimension_semantics=("arbitrary",) * n_axes, vmem_limit_bytes=VMEM_LIMIT)


def _const_spec(shape):
    nd = len(shape)
    return pl.BlockSpec(shape, lambda *_: (0,) * nd, pipeline_mode=pl.Buffered(1))


def _memkv_kernel(mem_ref, wk_ref, wv_ref, k_ref, v_ref):
    m = mem_ref[...]
    k_ref[0] = _dot(m, wk_ref[0]).astype(BF16)
    v_ref[0] = _dot(m, wv_ref[0]).astype(BF16)


def _memkv(mem2, wk, wv, mem_len):
    depth, d, _ = wk.shape
    rows = mem2.shape[0]
    nb = rows // mem_len
    return pl.pallas_call(
        _memkv_kernel,
        grid=(depth, nb),
        in_specs=[
            pl.BlockSpec((mem_len, d), lambda l, b: (b, 0)),
            pl.BlockSpec((1, d, d), lambda l, b: (l, 0, 0)),
            pl.BlockSpec((1, d, d), lambda l, b: (l, 0, 0)),
        ],
        out_specs=[
            pl.BlockSpec((1, mem_len, d), lambda l, b: (l, b, 0)),
            pl.BlockSpec((1, mem_len, d), lambda l, b: (l, b, 0)),
        ],
        out_shape=[jax.ShapeDtypeStruct((depth, rows, d), BF16)] * 2,
        compiler_params=_params(2),
        name="mem_kv",
    )(mem2, wk, wv)


def _inproj_kernel(x_ref, w_ref, u_ref, qkv_ref, *, cch, sbw):
    xb = x_ref[...].astype(BF16)
    ga = _dot(xb, w_ref[:, 0:cch])
    gg = _dot(xb, w_ref[:, cch:2 * cch])
    u_ref[...] = ga * jax.nn.sigmoid(gg)
    q = _dot(xb, w_ref[:, 2 * cch:2 * cch + sbw]) * (SB_HEAD_DIM ** -0.5)
    qkv_ref[:, 0:sbw] = q.astype(BF16)
    qkv_ref[:, sbw:3 * sbw] = _dot(xb, w_ref[:, 2 * cch + sbw:]).astype(BF16)


def _inproj(x2, w_in, cch, sbw):
    n, d = x2.shape
    return pl.pallas_call(
        functools.partial(_inproj_kernel, cch=cch, sbw=sbw),
        grid=(n // TM,),
        in_specs=[
            pl.BlockSpec((TM, d), lambda i: (i, 0)),
            _const_spec(w_in.shape),
        ],
        out_specs=[
            pl.BlockSpec((TM, cch), lambda i: (i, 0)),
            pl.BlockSpec((TM, 3 * sbw), lambda i: (i, 0)),
        ],
        out_shape=[
            jax.ShapeDtypeStruct((n, cch), F32),
            jax.ShapeDtypeStruct((n, 3 * sbw), BF16),
        ],
        compiler_params=_params(1),
        name="in_proj",
    )(x2, w_in)


def _neg_softplus(z):
    return -(jnp.maximum(z, 0.0) + jnp.log(1.0 + jnp.exp(-jnp.abs(z))))


def _split_bf16(t):
    hi = t.astype(BF16)
    return hi, (t - hi.astype(F32)).astype(BF16)


def _stack_heads(q_pair):
    lane = lax.broadcasted_iota(jnp.int32, q_pair.shape, 1)
    zero = jnp.zeros_like(q_pair)
    return jnp.concatenate([jnp.where(lane < SB_HEAD_DIM, q_pair, zero),
                            jnp.where(lane < SB_HEAD_DIM, zero, q_pair)], axis=0)


def _unstack_heads(o_stacked):
    rows = o_stacked.shape[0] // 2
    lane = lax.broadcasted_iota(jnp.int32, (rows, LANES), 1)
    return jnp.where(lane < SB_HEAD_DIM, o_stacked[:rows], o_stacked[rows:])


def _sb_attn_kernel(q_ref, k_ref, v_ref, tri_ref, o_ref,
                    hl_ref, zl_ref, s_ref, bias_ref, acc_ref, stick_ref, *, pairs):
    q0 = pl.program_id(1) * STEP_Q
    n_sub = STEP_Q // SQ
    rows = 2 * SQ
    col = lax.broadcasted_iota(jnp.int32, (rows, WIN), 1)
    q_off = lax.broadcasted_iota(jnp.int32, (rows, WIN), 0) % SQ

    win_start = []
    top = jnp.float32(-jnp.inf)
    for s in range(n_sub):
        qs0 = q0 + s * SQ
        ws = pl.multiple_of(jnp.maximum(qs0 + SQ - WIN, 0), SQ)
        win_start.append(ws)
        bias_ref[s] = jnp.where(ws + col < qs0 + q_off, 0.0, -1e30)
        for p in range(pairs):
            u = s * pairs + p
            lanes = slice(p * LANES, (p + 1) * LANES)
            q_st = _stack_heads(q_ref[s * SQ:(s + 1) * SQ, lanes])
            z = _dot_nt(q_st, k_ref[pl.ds(ws, WIN), lanes]) + bias_ref[s]
            log_keep = _neg_softplus(z)
            hi, lo = _split_bf16(log_keep)
            hl_ref[u * rows:(u + 1) * rows, 0:WIN] = hi
            hl_ref[u * rows:(u + 1) * rows, WIN:2 * WIN] = lo
            zl_ref[u * rows:(u + 1) * rows, :] = z + log_keep
    s_ref[...] = _dot(hl_ref[...], tri_ref[...])
    for s in range(n_sub):
        ws = win_start[s]
        for p in range(pairs):
            u = s * pairs + p
            lanes = slice(p * LANES, (p + 1) * LANES)
            later = s_ref[u * rows:(u + 1) * rows, :]
            a = jnp.exp(zl_ref[u * rows:(u + 1) * rows, :] + later).astype(BF16)
            o = _dot(a, v_ref[pl.ds(ws, WIN), lanes])
            o_ref[s * SQ:(s + 1) * SQ, lanes] = _unstack_heads(o).astype(BF16)
            left = jnp.max(later[:, 0:LANES], axis=0, keepdims=True)
            left = jnp.max(jnp.where(col[0:1, 0:LANES] == 0, left, -jnp.inf))
            keys_remain = q0 + s * SQ + SQ - WIN > 0
            top = jnp.maximum(top, jnp.where(keys_remain, left, -jnp.inf))

    @pl.when(top > LOG_STICK_FLOOR)
    def _():
        tri = jnp.concatenate([tri_ref[0:TK, 0:TK], tri_ref[TK:2 * TK, 0:TK]], axis=1)
        n_blocks = (q0 + STEP_Q) // TK
        q_pos = q0 + lax.broadcasted_iota(jnp.int32, (2 * STEP_Q, TK), 0) % STEP_Q
        k_off = lax.broadcasted_iota(jnp.int32, (2 * STEP_Q, TK), 1)
        for p in range(pairs):
            lanes = slice(p * LANES, (p + 1) * LANES)
            q_st = _stack_heads(q_ref[:, lanes])
            acc_ref[...] = jnp.zeros_like(acc_ref)
            stick_ref[...] = jnp.zeros_like(stick_ref)

            def cond(carry):
                j, live = carry
                return jnp.logical_and(j < n_blocks, live)

            def body(carry):
                j, _ = carry
                ks = pl.multiple_of(q0 + STEP_Q - (j + 1) * TK, TK)
                z = _dot_nt(q_st, k_ref[pl.ds(ks, TK), lanes])
                z = jnp.where(ks + k_off < q_pos, z, -1e30)
                log_keep = _neg_softplus(z)
                hi, lo = _split_bf16(log_keep)
                sums = _dot(hi, tri) + _dot(lo, tri)
                stick = stick_ref[...]
                a = jnp.exp(z + log_keep + sums[:, :TK] + stick)
                acc_ref[...] += _dot(a.astype(BF16), v_ref[pl.ds(ks, TK), lanes])
                stick = stick + sums[:, TK:]
                stick_ref[...] = stick
                return j + 1, jnp.max(stick) > LOG_STICK_FLOOR

            lax.while_loop(cond, body, (jnp.int32(0), jnp.bool_(True)))
            o_ref[:, lanes] = _unstack_heads(acc_ref[...]).astype(BF16)


def _sb_attention(qkv, bsz, seq, sbw):
    n = qkv.shape[0]
    pairs = sbw // LANES
    steps = seq // STEP_Q
    units = (STEP_Q // SQ) * pairs
    r = lax.broadcasted_iota(jnp.int32, (2 * WIN, WIN), 0) % WIN
    c = lax.broadcasted_iota(jnp.int32, (2 * WIN, WIN), 1)
    tri = jnp.where(r > c, 1.0, 0.0).astype(BF16)
    return pl.pallas_call(
        functools.partial(_sb_attn_kernel, pairs=pairs),
        grid=(bsz, steps),
        in_specs=[
            pl.BlockSpec((STEP_Q, sbw), lambda b, i: (b * steps + i, 0)),
            pl.BlockSpec((seq, sbw), lambda b, i: (b, 1)),
            pl.BlockSpec((seq, sbw), lambda b, i: (b, 2)),
            _const_spec((2 * WIN, WIN)),
        ],
        out_specs=pl.BlockSpec((STEP_Q, sbw), lambda b, i: (b * steps + i, 0)),
        out_shape=jax.ShapeDtypeStruct((n, sbw), BF16),
        scratch_shapes=[
            pltpu.VMEM((units * 2 * SQ, 2 * WIN), BF16),
            pltpu.VMEM((units * 2 * SQ, WIN), F32),
            pltpu.VMEM((units * 2 * SQ, WIN), F32),
            pltpu.VMEM((STEP_Q // SQ, 2 * SQ, WIN), F32),
            pltpu.VMEM((2 * STEP_Q, LANES), F32),
            pltpu.VMEM((2 * STEP_Q, LANES), F32),
        ],
        compiler_params=_params(2),
        name="sb_attn",
    )(qkv, qkv, qkv, tri)


def _mix_out_kernel(u_ref, uh_ref, a_ref, x_ref, cw_ref, cb_ref, cg_ref, cbeta_ref,
                    wo_ref, g_ref, b_ref, o_ref, ext_ref, *, tiles_per_seq, alpha, cch):
    at_start = (pl.program_id(0) % tiles_per_seq) == 0
    ext_ref[0:CONV_HALO] = jnp.where(at_start, 0.0, uh_ref[...])
    ext_ref[CONV_HALO:CONV_HALO + TM] = u_ref[...]
    y = jnp.zeros((TM, cch), F32) + cb_ref[...]
    for k in range(CONV_K):
        off = CONV_HALO - (CONV_K - 1) + k
        y = y + cw_ref[k:k + 1, :] * ext_ref[off:off + TM, :]
    yn = _layer_norm(y, cg_ref[...], cbeta_ref[...])
    act = yn * jax.nn.sigmoid(yn)
    mix = _dot(act.astype(BF16), wo_ref[0:cch, :]) + _dot(a_ref[...], wo_ref[cch:, :])
    o_ref[...] = _layer_norm(alpha * x_ref[...] + mix, g_ref[...], b_ref[...])


def _mix_out(u, a, x2, cw, cb, cg, cbeta, w_out, g, b, seq, alpha):
    n, d = x2.shape
    cch = u.shape[1]
    tiles_per_seq = seq // TM
    halo_blocks = TM // CONV_HALO
    row = lambda v: v.reshape(1, -1)
    return pl.pallas_call(
        functools.partial(_mix_out_kernel, tiles_per_seq=tiles_per_seq, alpha=alpha, cch=cch),
        grid=(n // TM,),
        in_specs=[
            pl.BlockSpec((TM, cch), lambda i: (i, 0)),
            pl.BlockSpec((CONV_HALO, cch), lambda i: (jnp.maximum(i * halo_blocks - 1, 0), 0)),
            pl.BlockSpec((TM, a.shape[1]), lambda i: (i, 0)),
            pl.BlockSpec((TM, d), lambda i: (i, 0)),
            _const_spec(cw.shape),
            _const_spec((1, cch)), _const_spec((1, cch)), _const_spec((1, cch)),
            _const_spec(w_out.shape),
            _const_spec((1, d)), _const_spec((1, d)),
        ],
        out_specs=pl.BlockSpec((TM, d), lambda i: (i, 0)),
        out_shape=jax.ShapeDtypeStruct((n, d), F32),
        scratch_shapes=[pltpu.VMEM((CONV_HALO + TM, cch), F32)],
        compiler_params=_params(1),
        name="mix_out",
    )(u, u, a, x2, cw, row(cb), row(cg), row(cbeta), w_out, row(g), row(b))


def _cross_kernel(x_ref, wq_ref, k_ref, v_ref, wo_ref, g_ref, b_ref, o_ref, *, alpha, dh):
    x = x_ref[...]
    q = (_dot(x.astype(BF16), wq_ref[...]) * (dh ** -0.5)).astype(BF16)
    outs = []
    for h in range(MEM_HEADS):
        sl = slice(h * dh, (h + 1) * dh)
        s = _dot_nt(q[:, sl], k_ref[:, sl])
        e = jnp.exp(s - jnp.max(s, axis=-1, keepdims=True))
        p = e / jnp.sum(e, axis=-1, keepdims=True)
        outs.append(_dot(p.astype(BF16), v_ref[:, sl]).astype(BF16))
    cross = _dot(jnp.concatenate(outs, axis=-1), wo_ref[...])
    o_ref[...] = _layer_norm(alpha * x + cross, g_ref[...], b_ref[...])


def _cross(x2, wq, k, v, wo, g, b, seq, mem_len, alpha):
    n, d = x2.shape
    tiles_per_seq = seq // TM
    row = lambda t: t.reshape(1, -1)
    return pl.pallas_call(
        functools.partial(_cross_kernel, alpha=alpha, dh=d // MEM_HEADS),
        grid=(n // TM,),
        in_specs=[
            pl.BlockSpec((TM, d), lambda i: (i, 0)),
            _const_spec(wq.shape),
            pl.BlockSpec((mem_len, d), lambda i: (i // tiles_per_seq, 0)),
            pl.BlockSpec((mem_len, d), lambda i: (i // tiles_per_seq, 0)),
            _const_spec(wo.shape),
            _const_spec((1, d)), _const_spec((1, d)),
        ],
        out_specs=pl.BlockSpec((TM, d), lambda i: (i, 0)),
        out_shape=jax.ShapeDtypeStruct((n, d), F32),
        compiler_params=_params(1),
        name="cross_attn",
    )(x2, wq, k, v, wo, row(g), row(b))


def _ffn_kernel(x_ref, xh_ref, wup_ref, cw_ref, cb_ref, wdn_ref, g_ref, b_ref, o_ref,
                xe_ref, val_ref, gate_ref, acc_ref, *, tiles_per_seq, alpha, n_chunks):
    at_start = (pl.program_id(0) % tiles_per_seq) == 0
    x = x_ref[...]
    xe_ref[0:FFN_HALO] = jnp.where(at_start, 0.0, xh_ref[...]).astype(BF16)
    xe_ref[FFN_HALO:FFN_HALO + TM] = x.astype(BF16)
    acc_ref[...] = jnp.zeros_like(acc_ref)

    def conv(e_ref, w, bias):
        base = FFN_HALO - (FFN_K - 1)
        y = bias
        for k in range(FFN_K):
            y = y + w[k:k + 1, :] * e_ref[base + k:base + k + TM, :]
        return y

    def chunk(c, carry):
        xe = xe_ref[...]
        val_ref[...] = _dot(xe, wup_ref[c])
        gate_ref[...] = _dot(xe, wup_ref[n_chunks + c])
        val = conv(val_ref, cw_ref[c], cb_ref[c])
        gate = conv(gate_ref, cw_ref[n_chunks + c], cb_ref[n_chunks + c])
        hid = (gate * jax.nn.sigmoid(gate)) * val
        acc_ref[...] += _dot(hid.astype(BF16), wdn_ref[c])
        return carry

    lax.fori_loop(0, n_chunks, chunk, 0)
    o_ref[...] = _layer_norm(alpha * x + acc_ref[...], g_ref[...], b_ref[...])


def _ffn(x2, wup, cw, cb, wdn, g, b, seq, alpha):
    n, d = x2.shape
    n_chunks = wdn.shape[0]
    tiles_per_seq = seq // TM
    halo_blocks = TM // FFN_HALO
    row = lambda t: t.reshape(1, -1)
    return pl.pallas_call(
        functools.partial(_ffn_kernel, tiles_per_seq=tiles_per_seq, alpha=alpha, n_chunks=n_chunks),
        grid=(n // TM,),
        in_specs=[
            pl.BlockSpec((TM, d), lambda i: (i, 0)),
            pl.BlockSpec((FFN_HALO, d), lambda i: (jnp.maximum(i * halo_blocks - 1, 0), 0)),
            _const_spec(wup.shape),
            _const_spec(cw.shape),
            _const_spec(cb.shape),
            _const_spec(wdn.shape),
            _const_spec((1, d)), _const_spec((1, d)),
        ],
        out_specs=pl.BlockSpec((TM, d), lambda i: (i, 0)),
        out_shape=jax.ShapeDtypeStruct((n, d), F32),
        scratch_shapes=[
            pltpu.VMEM((FFN_HALO + TM, d), BF16),
            pltpu.VMEM((FFN_HALO + TM, FC), F32),
            pltpu.VMEM((FFN_HALO + TM, FC), F32),
            pltpu.VMEM((TM, d), F32),
        ],
        compiler_params=_params(1),
        name="conv_ffn",
    )(x2, x2, wup, cw, cb, wdn, row(g), row(b))


def _chunk_cols(t, n_chunks):
    lead = t.shape[:-2]
    r = t.shape[-2]
    t = t.reshape(*lead, r, n_chunks, FC)
    return jnp.moveaxis(t, -2, -3)


def kernel(x, mem, w_in, conv_w, conv_b, conv_ln_g, conv_ln_b, w_out, ln1_g, ln1_b,
           mem_wq, mem_wk, mem_wv, mem_wo, ln2_g, ln2_b,
           ffn_up, ffn_conv_w, ffn_conv_b, ffn_down, ln3_g, ln3_b):
    bsz, seq, d = x.shape
    depth = w_in.shape[0]
    mem_len = mem.shape[1]
    cch = conv_w.shape[2]
    sbw = SB_HEADS * SB_HEAD_DIM
    d_ff = ffn_down.shape[1]
    n_chunks = d_ff // FC
    alpha = (2.0 * depth) ** 0.25
    assert seq % TM == 0 and seq % STEP_Q == 0 and d_ff % FC == 0
    assert w_in.shape[2] == 2 * cch + 3 * sbw and sbw % LANES == 0

    w_in_b = w_in.astype(BF16)
    w_out_b = w_out.astype(BF16)
    wq_b = mem_wq.astype(BF16)
    wo_b = mem_wo.astype(BF16)
    wup_b = _chunk_cols(ffn_up.astype(BF16), 2 * n_chunks)
    fcw = _chunk_cols(ffn_conv_w, 2 * n_chunks)
    fcb = _chunk_cols(ffn_conv_b[:, None, :], 2 * n_chunks)
    wdn_b = ffn_down.astype(BF16).reshape(depth, n_chunks, FC, d)

    mem_k, mem_v = _memkv(mem.reshape(bsz * mem_len, d).astype(BF16),
                          mem_wk.astype(BF16), mem_wv.astype(BF16), mem_len)

    h = x.reshape(bsz * seq, d)
    for l in range(depth):
        u, qkv = _inproj(h, w_in_b[l], cch, sbw)
        a = _sb_attention(qkv, bsz, seq, sbw)
        h = _mix_out(u, a, h, conv_w[l], conv_b[l], conv_ln_g[l], conv_ln_b[l],
                     w_out_b[l], ln1_g[l], ln1_b[l], seq, alpha)
        h = _cross(h, wq_b[l], mem_k[l], mem_v[l], wo_b[l], ln2_g[l], ln2_b[l],
                   seq, mem_len, alpha)
        h = _ffn(h, wup_b[l], fcw[l], fcb[l], wdn_b[l], ln3_g[l], ln3_b[l], seq, alpha)
    return h.reshape(bsz, seq, d)
```

```python
import functools

import jax
import jax.numpy as jnp
from jax import lax
from jax.experimental import pallas as pl
from jax.experimental.pallas import tpu as pltpu

F32 = jnp.float32
BF16 = jnp.bfloat16

CONV_K = 31
SB_HEADS = 8
SB_HEAD_DIM = 64
MEM_HEADS = 4
FFN_K = 3
LN_EPS = 1e-5

LANES = 128
SUBLANES = 8
VMEM_LIMIT = 56 * 1024 * 1024

TM = 512
STEP_Q = 128
SQ = 64
WIN = 256
TK = 128
FC = 256
CONV_HALO = 32
FFN_HALO = 8
LOG_STICK_FLOOR = -100.0


def _dot(a, b):
    return jnp.dot(a, b, preferred_element_type=F32)


def _dot_nt(a, b):
    return lax.dot_general(a, b, (((1,), (1,)), ((), ())), preferred_element_type=F32)


def _layer_norm(h, g, b):
    mu = jnp.mean(h, axis=-1, keepdims=True)
    d = h - mu
    var = jnp.mean(d * d, axis=-1, keepdims=True)
    return d * lax.rsqrt(var + LN_EPS) * g + b


def _params(n_axes):
    return pltpu.CompilerParams(
        dimension_semantics=("arbitrary",) * n_axes, vmem_limit_bytes=VMEM_LIMIT)


def _const_spec(shape):
    nd = len(shape)
    return pl.BlockSpec(shape, lambda *_: (0,) * nd, pipeline_mode=pl.Buffered(1))


def _memkv_kernel(mem_ref, wk_ref, wv_ref, k_ref, v_ref):
    m = mem_ref[...]
    k_ref[0] = _dot(m, wk_ref[0]).astype(BF16)
    v_ref[0] = _dot(m, wv_ref[0]).astype(BF16)


def _memkv(mem2, wk, wv, mem_len):
    depth, d, _ = wk.shape
    rows = mem2.shape[0]
    nb = rows // mem_len
    return pl.pallas_call(
        _memkv_kernel,
        grid=(depth, nb),
        in_specs=[
            pl.BlockSpec((mem_len, d), lambda l, b: (b, 0)),
            pl.BlockSpec((1, d, d), lambda l, b: (l, 0, 0)),
            pl.BlockSpec((1, d, d), lambda l, b: (l, 0, 0)),
        ],
        out_specs=[
            pl.BlockSpec((1, mem_len, d), lambda l, b: (l, b, 0)),
            pl.BlockSpec((1, mem_len, d), lambda l, b: (l, b, 0)),
        ],
        out_shape=[jax.ShapeDtypeStruct((depth, rows, d), BF16)] * 2,
        compiler_params=_params(2),
        name="mem_kv",
    )(mem2, wk, wv)


def _inproj_kernel(x_ref, w_ref, u_ref, qkv_ref, *, cch, sbw):
    xb = x_ref[...].astype(BF16)
    ga = _dot(xb, w_ref[:, 0:cch])
    gg = _dot(xb, w_ref[:, cch:2 * cch])
    u_ref[...] = ga * jax.nn.sigmoid(gg)
    q = _dot(xb, w_ref[:, 2 * cch:2 * cch + sbw]) * (SB_HEAD_DIM ** -0.5)
    qkv_ref[:, 0:sbw] = q.astype(BF16)
    qkv_ref[:, sbw:3 * sbw] = _dot(xb, w_ref[:, 2 * cch + sbw:]).astype(BF16)


def _inproj(x2, w_in, cch, sbw):
    n, d = x2.shape
    return pl.pallas_call(
        functools.partial(_inproj_kernel, cch=cch, sbw=sbw),
        grid=(n // TM,),
        in_specs=[
            pl.BlockSpec((TM, d), lambda i: (i, 0)),
            _const_spec(w_in.shape),
        ],
        out_specs=[
            pl.BlockSpec((TM, cch), lambda i: (i, 0)),
            pl.BlockSpec((TM, 3 * sbw), lambda i: (i, 0)),
        ],
        out_shape=[
            jax.ShapeDtypeStruct((n, cch), F32),
            jax.ShapeDtypeStruct((n, 3 * sbw), BF16),
        ],
        compiler_params=_params(1),
        name="in_proj",
    )(x2, w_in)


def _neg_softplus(z):
    return -(jnp.maximum(z, 0.0) + jnp.log(1.0 + jnp.exp(-jnp.abs(z))))


def _split_bf16(t):
    hi = t.astype(BF16)
    return hi, (t - hi.astype(F32)).astype(BF16)


def _stack_heads(q_pair):
    lane = lax.broadcasted_iota(jnp.int32, q_pair.shape, 1)
    zero = jnp.zeros_like(q_pair)
    return jnp.concatenate([jnp.where(lane < SB_HEAD_DIM, q_pair, zero),
                            jnp.where(lane < SB_HEAD_DIM, zero, q_pair)], axis=0)


def _unstack_heads(o_stacked):
    rows = o_stacked.shape[0] // 2
    lane = lax.broadcasted_iota(jnp.int32, (rows, LANES), 1)
    return jnp.where(lane < SB_HEAD_DIM, o_stacked[:rows], o_stacked[rows:])


def _sb_attn_kernel(q_ref, k_ref, v_ref, tri_ref, o_ref,
                    hl_ref, zl_ref, s_ref, bias_ref, acc_ref, stick_ref, *, pairs):
    q0 = pl.program_id(1) * STEP_Q
    n_sub = STEP_Q // SQ
    rows = 2 * SQ
    col = lax.broadcasted_iota(jnp.int32, (rows, WIN), 1)
    q_off = lax.broadcasted_iota(jnp.int32, (rows, WIN), 0) % SQ

    win_start = []
    top = jnp.float32(-jnp.inf)
    for s in range(n_sub):
        qs0 = q0 + s * SQ
        ws = pl.multiple_of(jnp.maximum(qs0 + SQ - WIN, 0), SQ)
        win_start.append(ws)
        bias_ref[s] = jnp.where(ws + col < qs0 + q_off, 0.0, -1e30)
        for p in range(pairs):
            u = s * pairs + p
            lanes = slice(p * LANES, (p + 1) * LANES)
            q_st = _stack_heads(q_ref[s * SQ:(s + 1) * SQ, lanes])
            z = _dot_nt(q_st, k_ref[pl.ds(ws, WIN), lanes]) + bias_ref[s]
            log_keep = _neg_softplus(z)
            hi, lo = _split_bf16(log_keep)
            hl_ref[u * rows:(u + 1) * rows, 0:WIN] = hi
            hl_ref[u * rows:(u + 1) * rows, WIN:2 * WIN] = lo
            zl_ref[u * rows:(u + 1) * rows, :] = z + log_keep
    s_ref[...] = _dot(hl_ref[...], tri_ref[...])
    for s in range(n_sub):
        ws = win_start[s]
        for p in range(pairs):
            u = s * pairs + p
            lanes = slice(p * LANES, (p + 1) * LANES)
            later = s_ref[u * rows:(u + 1) * rows, :]
            a = jnp.exp(zl_ref[u * rows:(u + 1) * rows, :] + later).astype(BF16)
            o = _dot(a, v_ref[pl.ds(ws, WIN), lanes])
            o_ref[s * SQ:(s + 1) * SQ, lanes] = _unstack_heads(o).astype(BF16)
            left = jnp.max(later[:, 0:LANES], axis=0, keepdims=True)
            left = jnp.max(jnp.where(col[0:1, 0:LANES] == 0, left, -jnp.inf))
            keys_remain = q0 + s * SQ + SQ - WIN > 0
            top = jnp.maximum(top, jnp.where(keys_remain, left, -jnp.inf))

    @pl.when(top > LOG_STICK_FLOOR)
    def _():
        tri = jnp.concatenate([tri_ref[0:TK, 0:TK], tri_ref[TK:2 * TK, 0:TK]], axis=1)
        n_blocks = (q0 + STEP_Q) // TK
        q_pos = q0 + lax.broadcasted_iota(jnp.int32, (2 * STEP_Q, TK), 0) % STEP_Q
        k_off = lax.broadcasted_iota(jnp.int32, (2 * STEP_Q, TK), 1)
        for p in range(pairs):
            lanes = slice(p * LANES, (p + 1) * LANES)
            q_st = _stack_heads(q_ref[:, lanes])
            acc_ref[...] = jnp.zeros_like(acc_ref)
            stick_ref[...] = jnp.zeros_like(stick_ref)

            def cond(carry):
                j, live = carry
                return jnp.logical_and(j < n_blocks, live)

            def body(carry):
                j, _ = carry
                ks = pl.multiple_of(q0 + STEP_Q - (j + 1) * TK, TK)
                z = _dot_nt(q_st, k_ref[pl.ds(ks, TK), lanes])
                z = jnp.where(ks + k_off < q_pos, z, -1e30)
                log_keep = _neg_softplus(z)
                hi, lo = _split_bf16(log_keep)
                sums = _dot(hi, tri) + _dot(lo, tri)
                stick = stick_ref[...]
                a = jnp.exp(z + log_keep + sums[:, :TK] + stick)
                acc_ref[...] += _dot(a.astype(BF16), v_ref[pl.ds(ks, TK), lanes])
                stick = stick + sums[:, TK:]
                stick_ref[...] = stick
                return j + 1, jnp.max(stick) > LOG_STICK_FLOOR

            lax.while_loop(cond, body, (jnp.int32(0), jnp.bool_(True)))
            o_ref[:, lanes] = _unstack_heads(acc_ref[...]).astype(BF16)


def _sb_attention(qkv, bsz, seq, sbw):
    n = qkv.shape[0]
    pairs = sbw // LANES
    steps = seq // STEP_Q
    units = (STEP_Q // SQ) * pairs
    r = lax.broadcasted_iota(jnp.int32, (2 * WIN, WIN), 0) % WIN
    c = lax.broadcasted_iota(jnp.int32, (2 * WIN, WIN), 1)
    tri = jnp.where(r > c, 1.0, 0.0).astype(BF16)
    return pl.pallas_call(
        functools.partial(_sb_attn_kernel, pairs=pairs),
        grid=(bsz, steps),
        in_specs=[
            pl.BlockSpec((STEP_Q, sbw), lambda b, i: (b * steps + i, 0)),
            pl.BlockSpec((seq, sbw), lambda b, i: (b, 1)),
            pl.BlockSpec((seq, sbw), lambda b, i: (b, 2)),
            _const_spec((2 * WIN, WIN)),
        ],
        out_specs=pl.BlockSpec((STEP_Q, sbw), lambda b, i: (b * steps + i, 0)),
        out_shape=jax.ShapeDtypeStruct((n, sbw), BF16),
        scratch_shapes=[
            pltpu.VMEM((units * 2 * SQ, 2 * WIN), BF16),
            pltpu.VMEM((units * 2 * SQ, WIN), F32),
            pltpu.VMEM((units * 2 * SQ, WIN), F32),
            pltpu.VMEM((STEP_Q // SQ, 2 * SQ, WIN), F32),
            pltpu.VMEM((2 * STEP_Q, LANES), F32),
            pltpu.VMEM((2 * STEP_Q, LANES), F32),
        ],
        compiler_params=_params(2),
        name="sb_attn",
    )(qkv, qkv, qkv, tri)


RUN = TM // SUBLANES
PITCH = RUN + SUBLANES
STAGE_BASE = CONV_HALO + SUBLANES
STAGE_ROWS = STAGE_BASE + (SUBLANES - 1) * PITCH + RUN
CONV_GROUPS = (CONV_K - 1) + RUN
CONV_BLOCK = 16


def _mix_out_kernel(u_ref, uh_ref, a_ref, x_ref, cw_ref, cb_ref, cg_ref, cbeta_ref, unperm_ref,
                    wo_ref, g_ref, b_ref, o_ref, stage_ref, il_ref, y_ref,
                    *, tiles_per_seq, alpha, cch):
    at_start = (pl.program_id(0) % tiles_per_seq) == 0
    hist = CONV_K - 1
    for s in range(cch // LANES):
        lanes = slice(s * LANES, (s + 1) * LANES)
        stage_ref[s, 0:CONV_HALO, :] = jnp.where(at_start, 0.0, uh_ref[:, lanes])
        for j in range(SUBLANES):
            r0 = STAGE_BASE + j * PITCH
            stage_ref[s, r0:r0 + RUN, :] = u_ref[j * RUN:(j + 1) * RUN, lanes]
        for g in range(CONV_GROUPS):
            off = g - hist
            start = STAGE_BASE + off - (0 if off >= 0 else PITCH - RUN)
            il_ref[s, g * SUBLANES:(g + 1) * SUBLANES, :] = stage_ref[
                s, pl.ds(start, SUBLANES, stride=PITCH), :]

    blocks = RUN // CONV_BLOCK

    def conv_block(idx, carry):
        s = idx // blocks
        base = pl.multiple_of((idx % blocks) * (CONV_BLOCK * SUBLANES), CONV_BLOCK * SUBLANES)
        taps = [jnp.broadcast_to(cw_ref[s, k:k + 1, :], (SUBLANES, LANES)) for k in range(CONV_K)]
        ys = [jnp.broadcast_to(cb_ref[s], (SUBLANES, LANES))] * CONV_BLOCK
        for g in range(CONV_BLOCK + hist):
            v = il_ref[s, pl.ds(base + g * SUBLANES, SUBLANES), :]
            for i in range(CONV_BLOCK):
                if 0 <= g - i < CONV_K:
                    ys[i] = ys[i] + taps[g - i] * v
        for i in range(CONV_BLOCK):
            y_ref[s, pl.ds(base + i * SUBLANES, SUBLANES), :] = ys[i]
        return carry

    lax.fori_loop(0, (cch // LANES) * blocks, conv_block, 0)
    y = jnp.concatenate([y_ref[s] for s in range(cch // LANES)], axis=1)
    yn = _layer_norm(y, cg_ref[...], cbeta_ref[...])
    act = (yn * jax.nn.sigmoid(yn)).astype(BF16)
    act = _dot(unperm_ref[...], act).astype(BF16)
    mix = _dot(act, wo_ref[0:cch, :]) + _dot(a_ref[...], wo_ref[cch:, :])
    o_ref[...] = _layer_norm(alpha * x_ref[...] + mix, g_ref[...], b_ref[...])


def _lane_slabs(t):
    r, c = t.shape
    return t.reshape(r, c // LANES, LANES).transpose(1, 0, 2)


def _mix_out(u, a, x2, cw, cb, cg, cbeta, w_out, g, b, seq, alpha):
    n, d = x2.shape
    cch = u.shape[1]
    tiles_per_seq = seq // TM
    halo_blocks = TM // CONV_HALO
    row = lambda v: v.reshape(1, -1)
    t = lax.broadcasted_iota(jnp.int32, (TM, TM), 0)
    c = lax.broadcasted_iota(jnp.int32, (TM, TM), 1)
    unperm = jnp.where(c == SUBLANES * (t % RUN) + t // RUN, 1.0, 0.0).astype(BF16)
    slabs = cch // LANES
    return pl.pallas_call(
        functools.partial(_mix_out_kernel, tiles_per_seq=tiles_per_seq, alpha=alpha, cch=cch),
        grid=(n // TM,),
        in_specs=[
            pl.BlockSpec((TM, cch), lambda i: (i, 0)),
            pl.BlockSpec((CONV_HALO, cch), lambda i: (jnp.maximum(i * halo_blocks - 1, 0), 0)),
            pl.BlockSpec((TM, a.shape[1]), lambda i: (i, 0)),
            pl.BlockSpec((TM, d), lambda i: (i, 0)),
            _const_spec((slabs, CONV_K, LANES)),
            _const_spec((slabs, 1, LANES)), _const_spec((1, cch)), _const_spec((1, cch)),
            _const_spec((TM, TM)),
            _const_spec(w_out.shape),
            _const_spec((1, d)), _const_spec((1, d)),
        ],
        out_specs=pl.BlockSpec((TM, d), lambda i: (i, 0)),
        out_shape=jax.ShapeDtypeStruct((n, d), F32),
        scratch_shapes=[
            pltpu.VMEM((slabs, STAGE_ROWS, LANES), F32),
            pltpu.VMEM((slabs, CONV_GROUPS * SUBLANES, LANES), F32),
            pltpu.VMEM((slabs, TM, LANES), F32),
        ],
        compiler_params=_params(1),
        name="mix_out",
    )(u, u, a, x2, _lane_slabs(cw), _lane_slabs(cb[None, :]), row(cg), row(cbeta), unperm,
      w_out, row(g), row(b))


def _cross_kernel(x_ref, wq_ref, k_ref, v_ref, wo_ref, g_ref, b_ref, o_ref, *, alpha, dh):
    x = x_ref[...]
    q = (_dot(x.astype(BF16), wq_ref[...]) * (dh ** -0.5)).astype(BF16)
    outs = []
    for h in range(MEM_HEADS):
        sl = slice(h * dh, (h + 1) * dh)
        s = _dot_nt(q[:, sl], k_ref[:, sl])
        e = jnp.exp(s - jnp.max(s, axis=-1, keepdims=True))
        p = e / jnp.sum(e, axis=-1, keepdims=True)
        outs.append(_dot(p.astype(BF16), v_ref[:, sl]).astype(BF16))
    cross = _dot(jnp.concatenate(outs, axis=-1), wo_ref[...])
    o_ref[...] = _layer_norm(alpha * x + cross, g_ref[...], b_ref[...])


def _cross(x2, wq, k, v, wo, g, b, seq, mem_len, alpha):
    n, d = x2.shape
    tiles_per_seq = seq // TM
    row = lambda t: t.reshape(1, -1)
    return pl.pallas_call(
        functools.partial(_cross_kernel, alpha=alpha, dh=d // MEM_HEADS),
        grid=(n // TM,),
        in_specs=[
            pl.BlockSpec((TM, d), lambda i: (i, 0)),
            _const_spec(wq.shape),
            pl.BlockSpec((mem_len, d), lambda i: (i // tiles_per_seq, 0)),
            pl.BlockSpec((mem_len, d), lambda i: (i // tiles_per_seq, 0)),
            _const_spec(wo.shape),
            _const_spec((1, d)), _const_spec((1, d)),
        ],
        out_specs=pl.BlockSpec((TM, d), lambda i: (i, 0)),
        out_shape=jax.ShapeDtypeStruct((n, d), F32),
        compiler_params=_params(1),
        name="cross_attn",
    )(x2, wq, k, v, wo, row(g), row(b))


def _ffn_kernel(x_ref, xh_ref, wup_ref, cw_ref, cb_ref, wdn_ref, g_ref, b_ref, o_ref,
                xe_ref, acc_ref, *, tiles_per_seq, alpha, n_chunks):
    at_start = (pl.program_id(0) % tiles_per_seq) == 0
    d = x_ref.shape[1]
    xe_ref[0:FFN_HALO] = jnp.where(at_start, 0.0, xh_ref[...]).astype(BF16)
    xe_ref[FFN_HALO:FFN_HALO + TM] = x_ref[...].astype(BF16)

    def conv(e, w, bias):
        base = FFN_HALO - (FFN_K - 1)
        y = bias
        for k in range(FFN_K):
            y = y + w[k:k + 1, :] * e[base + k:base + k + TM, :]
        return y

    def hidden(c):
        xe = xe_ref[...]
        val = conv(_dot(xe, wup_ref[c]), cw_ref[c], cb_ref[c])
        g = n_chunks + c
        gate = conv(_dot(xe, wup_ref[g]), cw_ref[g], cb_ref[g])
        return ((gate * jax.nn.sigmoid(gate)) * val).astype(BF16)

    for c in range(0, n_chunks, 2):
        if c + 1 < n_chunks:
            hid = jnp.concatenate([hidden(c), hidden(c + 1)], axis=1)
            part = _dot(hid, wdn_ref[c:c + 2].reshape(2 * FC, d))
        else:
            part = _dot(hidden(c), wdn_ref[c])
        if c == 0:
            acc_ref[...] = part
        else:
            acc_ref[...] += part
    o_ref[...] = _layer_norm(alpha * x_ref[...] + acc_ref[...], g_ref[...], b_ref[...])


def _ffn(x2, wup, cw, cb, wdn, g, b, seq, alpha):
    n, d = x2.shape
    n_chunks = wdn.shape[0]
    tiles_per_seq = seq // TM
    halo_blocks = TM // FFN_HALO
    row = lambda t: t.reshape(1, -1)
    return pl.pallas_call(
        functools.partial(_ffn_kernel, tiles_per_seq=tiles_per_seq, alpha=alpha, n_chunks=n_chunks),
        grid=(n // TM,),
        in_specs=[
            pl.BlockSpec((TM, d), lambda i: (i, 0)),
            pl.BlockSpec((FFN_HALO, d), lambda i: (jnp.maximum(i * halo_blocks - 1, 0), 0)),
            _const_spec(wup.shape),
            _const_spec(cw.shape),
            _const_spec(cb.shape),
            _const_spec(wdn.shape),
            _const_spec((1, d)), _const_spec((1, d)),
        ],
        out_specs=pl.BlockSpec((TM, d), lambda i: (i, 0)),
        out_shape=jax.ShapeDtypeStruct((n, d), F32),
        scratch_shapes=[
            pltpu.VMEM((FFN_HALO + TM, d), BF16),
            pltpu.VMEM((TM, d), F32),
        ],
        compiler_params=_params(1),
        name="conv_ffn",
    )(x2, x2, wup, cw, cb, wdn, row(g), row(b))


def _chunk_cols(t, n_chunks):
    lead = t.shape[:-2]
    r = t.shape[-2]
    t = t.reshape(*lead, r, n_chunks, FC)
    return jnp.moveaxis(t, -2, -3)


def kernel(x, mem, w_in, conv_w, conv_b, conv_ln_g, conv_ln_b, w_out, ln1_g, ln1_b,
           mem_wq, mem_wk, mem_wv, mem_wo, ln2_g, ln2_b,
           ffn_up, ffn_conv_w, ffn_conv_b, ffn_down, ln3_g, ln3_b):
    bsz, seq, d = x.shape
    depth = w_in.shape[0]
    mem_len = mem.shape[1]
    cch = conv_w.shape[2]
    sbw = SB_HEADS * SB_HEAD_DIM
    d_ff = ffn_down.shape[1]
    n_chunks = d_ff // FC
    alpha = (2.0 * depth) ** 0.25
    assert seq % TM == 0 and seq % STEP_Q == 0 and d_ff % FC == 0
    assert w_in.shape[2] == 2 * cch + 3 * sbw and sbw % LANES == 0

    w_in_b = w_in.astype(BF16)
    w_out_b = w_out.astype(BF16)
    wq_b = mem_wq.astype(BF16)
    wo_b = mem_wo.astype(BF16)
    wup_b = _chunk_cols(ffn_up.astype(BF16), 2 * n_chunks)
    fcw = _chunk_cols(ffn_conv_w, 2 * n_chunks)
    fcb = _chunk_cols(ffn_conv_b[:, None, :], 2 * n_chunks)
    wdn_b = ffn_down.astype(BF16).reshape(depth, n_chunks, FC, d)

    mem_k, mem_v = _memkv(mem.reshape(bsz * mem_len, d).astype(BF16),
                          mem_wk.astype(BF16), mem_wv.astype(BF16), mem_len)

    h = x.reshape(bsz * seq, d)
    for l in range(depth):
        u, qkv = _inproj(h, w_in_b[l], cch, sbw)
        a = _sb_attention(qkv, bsz, seq, sbw)
        h = _mix_out(u, a, h, conv_w[l], conv_b[l], conv_ln_g[l], conv_ln_b[l],
                     w_out_b[l], ln1_g[l], ln1_b[l], seq, alpha)
        h = _cross(h, wq_b[l], mem_k[l], mem_v[l], wo_b[l], ln2_g[l], ln2_b[l],
                   seq, mem_len, alpha)
        h = _ffn(h, wup_b[l], fcw[l], fcb[l], wdn_b[l], ln3_g[l], ln3_b[l], seq, alpha)
    return h.reshape(bsz, seq, d)
```

```python
import functools

import jax
import jax.numpy as jnp
from jax import lax
from jax.experimental import pallas as pl
from jax.experimental.pallas import tpu as pltpu

F32 = jnp.float32
BF16 = jnp.bfloat16

CONV_K = 31
SB_HEADS = 8
SB_HEAD_DIM = 64
MEM_HEADS = 4
FFN_K = 3
LN_EPS = 1e-5

LANES = 128
SUBLANES = 8
VMEM_LIMIT = 56 * 1024 * 1024

TM = 512
STEP_Q = 256
SQ = 64
WIN = 256
TK = 128
FC = 256
CONV_HALO = 32
FFN_HALO = 8
FFN_NORM_PARTS = 8
LOG_STICK_FLOOR = -100.0


def _dot(a, b):
    return jnp.dot(a, b, preferred_element_type=F32)


def _dot_nt(a, b):
    return lax.dot_general(a, b, (((1,), (1,)), ((), ())), preferred_element_type=F32)


def _layer_norm(h, g, b):
    mu = jnp.mean(h, axis=-1, keepdims=True)
    d = h - mu
    var = jnp.mean(d * d, axis=-1, keepdims=True)
    return d * lax.rsqrt(var + LN_EPS) * g + b


def _params(n_axes):
    return pltpu.CompilerParams(
        dimension_semantics=("arbitrary",) * n_axes, vmem_limit_bytes=VMEM_LIMIT)


def _const_spec(shape):
    nd = len(shape)
    return pl.BlockSpec(shape, lambda *_: (0,) * nd, pipeline_mode=pl.Buffered(1))


def _layer_spec(stacked, layer):
    tail = stacked.shape[1:]
    return pl.BlockSpec((None,) + tail, lambda *_: (layer,) + (0,) * len(tail),
                        pipeline_mode=pl.Buffered(1))


def _rows(stacked):
    return stacked[:, None, :]


def _memkv_kernel(mem_ref, wk_ref, wv_ref, k_ref, v_ref):
    m = mem_ref[...]
    k_ref[0] = _dot(m, wk_ref[0]).astype(BF16)
    v_ref[0] = _dot(m, wv_ref[0]).astype(BF16)


def _memkv(mem2, wk, wv, mem_len):
    depth, d, _ = wk.shape
    rows = mem2.shape[0]
    nb = rows // mem_len
    return pl.pallas_call(
        _memkv_kernel,
        grid=(depth, nb),
        in_specs=[
            pl.BlockSpec((mem_len, d), lambda l, b: (b, 0)),
            pl.BlockSpec((1, d, d), lambda l, b: (l, 0, 0)),
            pl.BlockSpec((1, d, d), lambda l, b: (l, 0, 0)),
        ],
        out_specs=[
            pl.BlockSpec((1, mem_len, d), lambda l, b: (l, b, 0)),
            pl.BlockSpec((1, mem_len, d), lambda l, b: (l, b, 0)),
        ],
        out_shape=[jax.ShapeDtypeStruct((depth, rows, d), BF16)] * 2,
        compiler_params=_params(2),
        name="mem_kv",
    )(mem2, wk, wv)


def _inproj_kernel(x_ref, w_ref, u_ref, qkv_ref, *, cch, sbw):
    xb = x_ref[...].astype(BF16)
    ga = _dot(xb, w_ref[:, 0:cch])
    gg = _dot(xb, w_ref[:, cch:2 * cch])
    u_ref[...] = ga * jax.nn.sigmoid(gg)
    q = _dot(xb, w_ref[:, 2 * cch:2 * cch + sbw]) * (SB_HEAD_DIM ** -0.5)
    qkv_ref[:, 0:sbw] = q.astype(BF16)
    qkv_ref[:, sbw:3 * sbw] = _dot(xb, w_ref[:, 2 * cch + sbw:]).astype(BF16)


def _inproj(x2, w_in, layer, cch, sbw):
    n, d = x2.shape
    return pl.pallas_call(
        functools.partial(_inproj_kernel, cch=cch, sbw=sbw),
        grid=(n // TM,),
        in_specs=[
            pl.BlockSpec((TM, d), lambda i: (i, 0)),
            _layer_spec(w_in, layer),
        ],
        out_specs=[
            pl.BlockSpec((TM, cch), lambda i: (i, 0)),
            pl.BlockSpec((TM, 3 * sbw), lambda i: (i, 0)),
        ],
        out_shape=[
            jax.ShapeDtypeStruct((n, cch), F32),
            jax.ShapeDtypeStruct((n, 3 * sbw), BF16),
        ],
        compiler_params=_params(1),
        name="in_proj",
    )(x2, w_in)


def _neg_softplus(z):
    return -(jnp.maximum(z, 0.0) + jnp.log(1.0 + jnp.exp(-jnp.abs(z))))


def _split_bf16(t):
    hi = t.astype(BF16)
    return hi, (t - hi.astype(F32)).astype(BF16)


def _stack_heads(q_pair):
    lane = lax.broadcasted_iota(jnp.int32, q_pair.shape, 1)
    zero = jnp.zeros_like(q_pair)
    return jnp.concatenate([jnp.where(lane < SB_HEAD_DIM, q_pair, zero),
                            jnp.where(lane < SB_HEAD_DIM, zero, q_pair)], axis=0)


def _unstack_heads(o_stacked):
    rows = o_stacked.shape[0] // 2
    lane = lax.broadcasted_iota(jnp.int32, (rows, LANES), 1)
    return jnp.where(lane < SB_HEAD_DIM, o_stacked[:rows], o_stacked[rows:])


def _sb_attn_kernel(q_ref, k_ref, v_ref, tri_ref, o_ref,
                    hl_ref, zl_ref, s_ref, bias_ref, acc_ref, stick_ref, *, pairs):
    q0 = pl.program_id(1) * STEP_Q
    n_sub = STEP_Q // SQ
    rows = 2 * SQ
    col = lax.broadcasted_iota(jnp.int32, (rows, WIN), 1)
    q_off = lax.broadcasted_iota(jnp.int32, (rows, WIN), 0) % SQ

    win_start = []
    top = jnp.float32(-jnp.inf)
    for s in range(n_sub):
        qs0 = q0 + s * SQ
        ws = pl.multiple_of(jnp.maximum(qs0 + SQ - WIN, 0), SQ)
        win_start.append(ws)
        bias_ref[s] = jnp.where(ws + col < qs0 + q_off, 0.0, -1e30)
        for p in range(pairs):
            u = s * pairs + p
            lanes = slice(p * LANES, (p + 1) * LANES)
            q_st = _stack_heads(q_ref[s * SQ:(s + 1) * SQ, lanes])
            z = _dot_nt(q_st, k_ref[pl.ds(ws, WIN), lanes]) + bias_ref[s]
            log_keep = _neg_softplus(z)
            hi, lo = _split_bf16(log_keep)
            hl_ref[u * rows:(u + 1) * rows, 0:WIN] = hi
            hl_ref[u * rows:(u + 1) * rows, WIN:2 * WIN] = lo
            zl_ref[u * rows:(u + 1) * rows, :] = z + log_keep
    s_ref[...] = _dot(hl_ref[...], tri_ref[...])
    for s in range(n_sub):
        ws = win_start[s]
        for p in range(pairs):
            u = s * pairs + p
            lanes = slice(p * LANES, (p + 1) * LANES)
            later = s_ref[u * rows:(u + 1) * rows, :]
            a = jnp.exp(zl_ref[u * rows:(u + 1) * rows, :] + later).astype(BF16)
            o = _dot(a, v_ref[pl.ds(ws, WIN), lanes])
            o_ref[s * SQ:(s + 1) * SQ, lanes] = _unstack_heads(o).astype(BF16)
            left = jnp.max(later[:, 0:LANES], axis=0, keepdims=True)
            left = jnp.max(jnp.where(col[0:1, 0:LANES] == 0, left, -jnp.inf))
            keys_remain = q0 + s * SQ + SQ - WIN > 0
            top = jnp.maximum(top, jnp.where(keys_remain, left, -jnp.inf))

    @pl.when(top > LOG_STICK_FLOOR)
    def _():
        tri = jnp.concatenate([tri_ref[0:TK, 0:TK], tri_ref[TK:2 * TK, 0:TK]], axis=1)
        n_blocks = (q0 + STEP_Q) // TK
        q_pos = q0 + lax.broadcasted_iota(jnp.int32, (2 * STEP_Q, TK), 0) % STEP_Q
        k_off = lax.broadcasted_iota(jnp.int32, (2 * STEP_Q, TK), 1)
        for p in range(pairs):
            lanes = slice(p * LANES, (p + 1) * LANES)
            q_st = _stack_heads(q_ref[:, lanes])
            acc_ref[...] = jnp.zeros_like(acc_ref)
            stick_ref[...] = jnp.zeros_like(stick_ref)

            def cond(carry):
                j, live = carry
                return jnp.logical_and(j < n_blocks, live)

            def body(carry):
                j, _ = carry
                ks = pl.multiple_of(q0 + STEP_Q - (j + 1) * TK, TK)
                z = _dot_nt(q_st, k_ref[pl.ds(ks, TK), lanes])
                z = jnp.where(ks + k_off < q_pos, z, -1e30)
                log_keep = _neg_softplus(z)
                hi, lo = _split_bf16(log_keep)
                sums = _dot(hi, tri) + _dot(lo, tri)
                stick = stick_ref[...]
                a = jnp.exp(z + log_keep + sums[:, :TK] + stick)
                acc_ref[...] += _dot(a.astype(BF16), v_ref[pl.ds(ks, TK), lanes])
                stick = stick + sums[:, TK:]
                stick_ref[...] = stick
                return j + 1, jnp.max(stick) > LOG_STICK_FLOOR

            lax.while_loop(cond, body, (jnp.int32(0), jnp.bool_(True)))
            o_ref[:, lanes] = _unstack_heads(acc_ref[...]).astype(BF16)


def _sb_attention(qkv, bsz, seq, sbw):
    n = qkv.shape[0]
    pairs = sbw // LANES
    steps = seq // STEP_Q
    units = (STEP_Q // SQ) * pairs
    r = lax.broadcasted_iota(jnp.int32, (2 * WIN, WIN), 0) % WIN
    c = lax.broadcasted_iota(jnp.int32, (2 * WIN, WIN), 1)
    tri = jnp.where(r > c, 1.0, 0.0).astype(BF16)
    return pl.pallas_call(
        functools.partial(_sb_attn_kernel, pairs=pairs),
        grid=(bsz, steps),
        in_specs=[
            pl.BlockSpec((STEP_Q, sbw), lambda b, i: (b * steps + i, 0)),
            pl.BlockSpec((seq, sbw), lambda b, i: (b, 1)),
            pl.BlockSpec((seq, sbw), lambda b, i: (b, 2)),
            _const_spec((2 * WIN, WIN)),
        ],
        out_specs=pl.BlockSpec((STEP_Q, sbw), lambda b, i: (b * steps + i, 0)),
        out_shape=jax.ShapeDtypeStruct((n, sbw), BF16),
        scratch_shapes=[
            pltpu.VMEM((units * 2 * SQ, 2 * WIN), BF16),
            pltpu.VMEM((units * 2 * SQ, WIN), F32),
            pltpu.VMEM((units * 2 * SQ, WIN), F32),
            pltpu.VMEM((STEP_Q // SQ, 2 * SQ, WIN), F32),
            pltpu.VMEM((2 * STEP_Q, LANES), F32),
            pltpu.VMEM((2 * STEP_Q, LANES), F32),
        ],
        compiler_params=_params(2),
        name="sb_attn",
    )(qkv, qkv, qkv, tri)


RUN = TM // SUBLANES
PITCH = RUN + SUBLANES
STAGE_BASE = CONV_HALO + SUBLANES
STAGE_ROWS = STAGE_BASE + (SUBLANES - 1) * PITCH + RUN
CONV_GROUPS = (CONV_K - 1) + RUN
CONV_BLOCK = 16


def _mix_out_kernel(u_ref, uh_ref, a_ref, x_ref, cw_ref, cb_ref, cg_ref, cbeta_ref, unperm_ref,
                    wo_ref, g_ref, b_ref, o_ref, stage_ref, il_ref, y_ref,
                    *, tiles_per_seq, alpha, cch):
    at_start = (pl.program_id(0) % tiles_per_seq) == 0
    hist = CONV_K - 1
    for s in range(cch // LANES):
        lanes = slice(s * LANES, (s + 1) * LANES)
        stage_ref[s, 0:CONV_HALO, :] = jnp.where(at_start, 0.0, uh_ref[:, lanes])
        for j in range(SUBLANES):
            r0 = STAGE_BASE + j * PITCH
            stage_ref[s, r0:r0 + RUN, :] = u_ref[j * RUN:(j + 1) * RUN, lanes]
        for g in range(CONV_GROUPS):
            off = g - hist
            start = STAGE_BASE + off - (0 if off >= 0 else PITCH - RUN)
            il_ref[s, g * SUBLANES:(g + 1) * SUBLANES, :] = stage_ref[
                s, pl.ds(start, SUBLANES, stride=PITCH), :]

    blocks = RUN // CONV_BLOCK

    def conv_block(idx, carry):
        s = idx // blocks
        base = pl.multiple_of((idx % blocks) * (CONV_BLOCK * SUBLANES), CONV_BLOCK * SUBLANES)
        taps = [jnp.broadcast_to(cw_ref[s, k:k + 1, :], (SUBLANES, LANES)) for k in range(CONV_K)]
        ys = [jnp.broadcast_to(cb_ref[s], (SUBLANES, LANES))] * CONV_BLOCK
        for g in range(CONV_BLOCK + hist):
            v = il_ref[s, pl.ds(base + g * SUBLANES, SUBLANES), :]
            for i in range(CONV_BLOCK):
                if 0 <= g - i < CONV_K:
                    ys[i] = ys[i] + taps[g - i] * v
        for i in range(CONV_BLOCK):
            y_ref[s, pl.ds(base + i * SUBLANES, SUBLANES), :] = ys[i]
        return carry

    lax.fori_loop(0, (cch // LANES) * blocks, conv_block, 0)
    y = jnp.concatenate([y_ref[s] for s in range(cch // LANES)], axis=1)
    yn = _layer_norm(y, cg_ref[...], cbeta_ref[...])
    act = (yn * jax.nn.sigmoid(yn)).astype(BF16)
    act = _dot(unperm_ref[...], act).astype(BF16)
    mix = _dot(act, wo_ref[0:cch, :]) + _dot(a_ref[...], wo_ref[cch:, :])
    o_ref[...] = _layer_norm(alpha * x_ref[...] + mix, g_ref[...], b_ref[...])


def _lane_slabs(t):
    depth, r, c = t.shape
    return t.reshape(depth, r, c // LANES, LANES).transpose(0, 2, 1, 3)


def _unpermute_matrix():
    t = lax.broadcasted_iota(jnp.int32, (TM, TM), 0)
    c = lax.broadcasted_iota(jnp.int32, (TM, TM), 1)
    return jnp.where(c == SUBLANES * (t % RUN) + t // RUN, 1.0, 0.0).astype(BF16)


def _mix_out(u, a, x2, layer, cw_slabs, cb_slabs, cg, cbeta, unperm, w_out, g, b, seq, alpha):
    n, d = x2.shape
    cch = u.shape[1]
    tiles_per_seq = seq // TM
    halo_blocks = TM // CONV_HALO
    slabs = cch // LANES
    return pl.pallas_call(
        functools.partial(_mix_out_kernel, tiles_per_seq=tiles_per_seq, alpha=alpha, cch=cch),
        grid=(n // TM,),
        in_specs=[
            pl.BlockSpec((TM, cch), lambda i: (i, 0)),
            pl.BlockSpec((CONV_HALO, cch), lambda i: (jnp.maximum(i * halo_blocks - 1, 0), 0)),
            pl.BlockSpec((TM, a.shape[1]), lambda i: (i, 0)),
            pl.BlockSpec((TM, d), lambda i: (i, 0)),
            _layer_spec(cw_slabs, layer), _layer_spec(cb_slabs, layer),
            _layer_spec(cg, layer), _layer_spec(cbeta, layer),
            _const_spec((TM, TM)),
            _layer_spec(w_out, layer),
            _layer_spec(g, layer), _layer_spec(b, layer),
        ],
        out_specs=pl.BlockSpec((TM, d), lambda i: (i, 0)),
        out_shape=jax.ShapeDtypeStruct((n, d), F32),
        scratch_shapes=[
            pltpu.VMEM((slabs, STAGE_ROWS, LANES), F32),
            pltpu.VMEM((slabs, CONV_GROUPS * SUBLANES, LANES), F32),
            pltpu.VMEM((slabs, TM, LANES), F32),
        ],
        compiler_params=_params(1),
        name="mix_out",
    )(u, u, a, x2, cw_slabs, cb_slabs, cg, cbeta, unperm, w_out, g, b)


def _norm_previous_tile(h_ref, g_ref, b_ref, o_ref, parts):
    i = pl.program_id(0)
    rows = h_ref.shape[1] // parts

    @pl.when(i == 0)
    def _():
        h_ref[1] = jnp.zeros(h_ref.shape[1:], h_ref.dtype)

    zeros = []
    for k in range(parts):
        out = _layer_norm(h_ref[(i + 1) % 2, k * rows:(k + 1) * rows, :], g_ref[...], b_ref[...])
        o_ref[k * rows:(k + 1) * rows, :] = out
        zeros.append(jnp.sum(out, keepdims=True) * 0.0)
    return i % 2, zeros


def _cross_kernel(x_ref, wq_ref, k_ref, v_ref, wo_ref, g_ref, b_ref, o_ref, h_ref, *, alpha, dh):
    slot, zeros = _norm_previous_tile(h_ref, g_ref, b_ref, o_ref, MEM_HEADS)
    x = x_ref[...]
    q = (_dot(x.astype(BF16), wq_ref[...]) * (dh ** -0.5)).astype(BF16)
    outs = []
    for h in range(MEM_HEADS):
        sl = slice(h * dh, (h + 1) * dh)
        s = _dot_nt(q[:, sl], k_ref[:, sl])
        e = jnp.exp(s - (jnp.max(s, axis=-1, keepdims=True) + zeros[h]))
        p = e / jnp.sum(e, axis=-1, keepdims=True)
        outs.append(_dot(p.astype(BF16), v_ref[:, sl]).astype(BF16))
    cross = _dot(jnp.concatenate(outs, axis=-1), wo_ref[...])
    h_ref[slot] = alpha * x + cross


def _cross(x2, layer, wq, k, v, wo, g, b, seq, mem_len, alpha):
    n, d = x2.shape
    tiles = n // TM
    tiles_per_seq = seq // TM
    cur = lambda i: jnp.minimum(i, tiles - 1)
    return pl.pallas_call(
        functools.partial(_cross_kernel, alpha=alpha, dh=d // MEM_HEADS),
        grid=(tiles + 1,),
        in_specs=[
            pl.BlockSpec((TM, d), lambda i: (cur(i), 0)),
            _layer_spec(wq, layer),
            pl.BlockSpec((None, mem_len, d), lambda i: (layer, cur(i) // tiles_per_seq, 0)),
            pl.BlockSpec((None, mem_len, d), lambda i: (layer, cur(i) // tiles_per_seq, 0)),
            _layer_spec(wo, layer),
            _layer_spec(g, layer), _layer_spec(b, layer),
        ],
        out_specs=pl.BlockSpec((TM, d), lambda i: (jnp.maximum(i - 1, 0), 0)),
        out_shape=jax.ShapeDtypeStruct((n, d), F32),
        scratch_shapes=[pltpu.VMEM((2, TM, d), F32)],
        compiler_params=_params(1),
        name="cross_attn",
    )(x2, wq, k, v, wo, g, b)


def _ffn_kernel(x_ref, xh_ref, wup_ref, cw_ref, cb_ref, wdn_ref, g_ref, b_ref, o_ref,
                xe_ref, up_ref, acc_ref, h_ref, *, tiles, tiles_per_seq, alpha, n_chunks):
    slot, zeros = _norm_previous_tile(h_ref, g_ref, b_ref, o_ref, FFN_NORM_PARTS)
    at_start = (jnp.minimum(pl.program_id(0), tiles - 1) % tiles_per_seq) == 0
    d = x_ref.shape[1]
    xe_ref[0:FFN_HALO] = jnp.where(at_start, 0.0, xh_ref[...]).astype(BF16)
    xe_ref[FFN_HALO:FFN_HALO + TM] = x_ref[...].astype(BF16)

    d_ff = n_chunks * FC
    val_cols = lambda c: slice(c * FC, (c + 1) * FC)
    gate_cols = lambda c: slice(d_ff + c * FC, d_ff + (c + 1) * FC)

    def project_up(c):
        xe = xe_ref[...]
        up_ref[c % 2, 0] = _dot(xe, wup_ref[:, val_cols(c)])
        up_ref[c % 2, 1] = _dot(xe, wup_ref[:, gate_cols(c)])

    def conv(e_ref, cols, bias):
        base = FFN_HALO - (FFN_K - 1)
        y = bias
        for k in range(FFN_K):
            y = y + cw_ref[k:k + 1, cols] * e_ref[base + k:base + k + TM, :]
        return y

    def hidden(c):
        bias = cb_ref[:, val_cols(c)]
        if c < FFN_NORM_PARTS:
            bias = bias + zeros[c]
        val = conv(up_ref.at[c % 2, 0], val_cols(c), bias)
        gate = conv(up_ref.at[c % 2, 1], gate_cols(c), cb_ref[:, gate_cols(c)])
        return ((gate * jax.nn.sigmoid(gate)) * val).astype(BF16)

    project_up(0)
    hids = []
    for c in range(n_chunks):
        if c + 1 < n_chunks:
            project_up(c + 1)
        hids.append(hidden(c))
        if len(hids) == 2 or c + 1 == n_chunks:
            first = c + 1 - len(hids)
            hid = hids[0] if len(hids) == 1 else jnp.concatenate(hids, axis=1)
            part = _dot(hid, wdn_ref[first * FC:(c + 1) * FC, :])
            if first == 0:
                acc_ref[...] = part
            else:
                acc_ref[...] += part
            hids = []
    h_ref[slot] = alpha * x_ref[...] + acc_ref[...]


def _ffn(x2, layer, wup, cw, cb, wdn, g, b, seq, alpha):
    n, d = x2.shape
    n_chunks = wdn.shape[1] // FC
    tiles_per_seq = seq // TM
    halo_blocks = TM // FFN_HALO
    tiles = n // TM
    cur = lambda i: jnp.minimum(i, tiles - 1)
    return pl.pallas_call(
        functools.partial(_ffn_kernel, tiles=tiles, tiles_per_seq=tiles_per_seq, alpha=alpha,
                          n_chunks=n_chunks),
        grid=(tiles + 1,),
        in_specs=[
            pl.BlockSpec((TM, d), lambda i: (cur(i), 0)),
            pl.BlockSpec((FFN_HALO, d), lambda i: (jnp.maximum(cur(i) * halo_blocks - 1, 0), 0)),
            _layer_spec(wup, layer),
            _layer_spec(cw, layer),
            _layer_spec(cb, layer),
            _layer_spec(wdn, layer),
            _layer_spec(g, layer), _layer_spec(b, layer),
        ],
        out_specs=pl.BlockSpec((TM, d), lambda i: (jnp.maximum(i - 1, 0), 0)),
        out_shape=jax.ShapeDtypeStruct((n, d), F32),
        scratch_shapes=[
            pltpu.VMEM((FFN_HALO + TM, d), BF16),
            pltpu.VMEM((2, 2, FFN_HALO + TM, FC), F32),
            pltpu.VMEM((TM, d), F32),
            pltpu.VMEM((2, TM, d), F32),
        ],
        compiler_params=_params(1),
        name="conv_ffn",
    )(x2, x2, wup, cw, cb, wdn, g, b)


def kernel(x, mem, w_in, conv_w, conv_b, conv_ln_g, conv_ln_b, w_out, ln1_g, ln1_b,
           mem_wq, mem_wk, mem_wv, mem_wo, ln2_g, ln2_b,
           ffn_up, ffn_conv_w, ffn_conv_b, ffn_down, ln3_g, ln3_b):
    bsz, seq, d = x.shape
    depth = w_in.shape[0]
    mem_len = mem.shape[1]
    cch = conv_w.shape[2]
    sbw = SB_HEADS * SB_HEAD_DIM
    d_ff = ffn_down.shape[1]
    alpha = (2.0 * depth) ** 0.25
    assert seq % TM == 0 and seq % STEP_Q == 0 and d_ff % FC == 0
    assert w_in.shape[2] == 2 * cch + 3 * sbw and sbw % LANES == 0

    w_in_b = w_in.astype(BF16)
    w_out_b = w_out.astype(BF16)
    wq_b = mem_wq.astype(BF16)
    wo_b = mem_wo.astype(BF16)
    wup_b = ffn_up.astype(BF16)
    wdn_b = ffn_down.astype(BF16)
    cw_slabs = _lane_slabs(conv_w)
    cb_slabs = _lane_slabs(_rows(conv_b))
    unperm = _unpermute_matrix()

    mem_k, mem_v = _memkv(mem.reshape(bsz * mem_len, d).astype(BF16),
                          mem_wk.astype(BF16), mem_wv.astype(BF16), mem_len)

    h = x.reshape(bsz * seq, d)
    for l in range(depth):
        u, qkv = _inproj(h, w_in_b, l, cch, sbw)
        a = _sb_attention(qkv, bsz, seq, sbw)
        h = _mix_out(u, a, h, l, cw_slabs, cb_slabs, _rows(conv_ln_g), _rows(conv_ln_b), unperm,
                     w_out_b, _rows(ln1_g), _rows(ln1_b), seq, alpha)
        h = _cross(h, l, wq_b, mem_k, mem_v, wo_b, _rows(ln2_g), _rows(ln2_b),
                   seq, mem_len, alpha)
        h = _ffn(h, l, wup_b, ffn_conv_w, _rows(ffn_conv_b), wdn_b, _rows(ln3_g), _rows(ln3_b),
                 seq, alpha)
    return h.reshape(bsz, seq, d)
```

```python
import functools

import jax
import jax.numpy as jnp
from jax import lax
from jax.experimental import pallas as pl
from jax.experimental.pallas import tpu as pltpu

F32 = jnp.float32
BF16 = jnp.bfloat16

CONV_K = 31
SB_HEADS = 8
SB_HEAD_DIM = 64
MEM_HEADS = 4
FFN_K = 3
LN_EPS = 1e-5

LANES = 128
SUBLANES = 8
VMEM_LIMIT = 56 * 1024 * 1024

TM = 512
STEP_Q = 512
SQ = 64
WIN = 256
TK = 128
FC = 256
CONV_HALO = 32
FFN_HALO = 8
FFN_NORM_PARTS = 8
LOG_STICK_FLOOR = -100.0


def _dot(a, b):
    return jnp.dot(a, b, preferred_element_type=F32)


def _dot_nt(a, b):
    return lax.dot_general(a, b, (((1,), (1,)), ((), ())), preferred_element_type=F32)


def _layer_norm(h, g, b):
    mu = jnp.mean(h, axis=-1, keepdims=True)
    d = h - mu
    var = jnp.mean(d * d, axis=-1, keepdims=True)
    return d * lax.rsqrt(var + LN_EPS) * g + b


def _params(n_axes):
    return pltpu.CompilerParams(
        dimension_semantics=("arbitrary",) * n_axes, vmem_limit_bytes=VMEM_LIMIT)


def _const_spec(shape):
    nd = len(shape)
    return pl.BlockSpec(shape, lambda *_: (0,) * nd, pipeline_mode=pl.Buffered(1))


def _layer_spec(stacked, layer):
    tail = stacked.shape[1:]
    return pl.BlockSpec((None,) + tail, lambda *_: (layer,) + (0,) * len(tail),
                        pipeline_mode=pl.Buffered(1))


def _rows(stacked):
    return stacked[:, None, :]


def _memkv_kernel(mem_ref, wk_ref, wv_ref, k_ref, v_ref):
    m = mem_ref[...]
    k_ref[0] = _dot(m, wk_ref[0]).astype(BF16)
    v_ref[0] = _dot(m, wv_ref[0]).astype(BF16)


def _memkv(mem2, wk, wv, mem_len):
    depth, d, _ = wk.shape
    rows = mem2.shape[0]
    nb = rows // mem_len
    return pl.pallas_call(
        _memkv_kernel,
        grid=(depth, nb),
        in_specs=[
            pl.BlockSpec((mem_len, d), lambda l, b: (b, 0)),
            pl.BlockSpec((1, d, d), lambda l, b: (l, 0, 0)),
            pl.BlockSpec((1, d, d), lambda l, b: (l, 0, 0)),
        ],
        out_specs=[
            pl.BlockSpec((1, mem_len, d), lambda l, b: (l, b, 0)),
            pl.BlockSpec((1, mem_len, d), lambda l, b: (l, b, 0)),
        ],
        out_shape=[jax.ShapeDtypeStruct((depth, rows, d), BF16)] * 2,
        compiler_params=_params(2),
        name="mem_kv",
    )(mem2, wk, wv)


def _inproj_kernel(x_ref, w_ref, u_ref, qkv_ref, *, cch, sbw):
    xb = x_ref[...].astype(BF16)
    ga = _dot(xb, w_ref[:, 0:cch])
    gg = _dot(xb, w_ref[:, cch:2 * cch])
    u_ref[...] = ga * jax.nn.sigmoid(gg)
    q = _dot(xb, w_ref[:, 2 * cch:2 * cch + sbw]) * (SB_HEAD_DIM ** -0.5)
    qkv_ref[:, 0:sbw] = q.astype(BF16)
    qkv_ref[:, sbw:3 * sbw] = _dot(xb, w_ref[:, 2 * cch + sbw:]).astype(BF16)


def _inproj(x2, w_in, layer, cch, sbw):
    n, d = x2.shape
    return pl.pallas_call(
        functools.partial(_inproj_kernel, cch=cch, sbw=sbw),
        grid=(n // TM,),
        in_specs=[
            pl.BlockSpec((TM, d), lambda i: (i, 0)),
            _layer_spec(w_in, layer),
        ],
        out_specs=[
            pl.BlockSpec((TM, cch), lambda i: (i, 0)),
            pl.BlockSpec((TM, 3 * sbw), lambda i: (i, 0)),
        ],
        out_shape=[
            jax.ShapeDtypeStruct((n, cch), F32),
            jax.ShapeDtypeStruct((n, 3 * sbw), BF16),
        ],
        compiler_params=_params(1),
        name="in_proj",
    )(x2, w_in)


def _neg_softplus(z):
    return -(jnp.maximum(z, 0.0) + jnp.log(1.0 + jnp.exp(-jnp.abs(z))))


def _split_bf16(t):
    hi = t.astype(BF16)
    return hi, (t - hi.astype(F32)).astype(BF16)


def _stack_heads(q_pair):
    lane = lax.broadcasted_iota(jnp.int32, q_pair.shape, 1)
    zero = jnp.zeros_like(q_pair)
    return jnp.concatenate([jnp.where(lane < SB_HEAD_DIM, q_pair, zero),
                            jnp.where(lane < SB_HEAD_DIM, zero, q_pair)], axis=0)


def _unstack_heads(o_stacked):
    rows = o_stacked.shape[0] // 2
    lane = lax.broadcasted_iota(jnp.int32, (rows, LANES), 1)
    return jnp.where(lane < SB_HEAD_DIM, o_stacked[:rows], o_stacked[rows:])


def _sb_attn_kernel(q_ref, k_ref, v_ref, tri_ref, o_ref,
                    hl_ref, zl_ref, s_ref, bias_ref, acc_ref, stick_ref, *, pairs):
    q0 = pl.program_id(1) * STEP_Q
    n_sub = STEP_Q // SQ
    rows = 2 * SQ
    col = lax.broadcasted_iota(jnp.int32, (rows, WIN), 1)
    q_off = lax.broadcasted_iota(jnp.int32, (rows, WIN), 0) % SQ

    win_start = []
    top = jnp.float32(-jnp.inf)
    for s in range(n_sub):
        qs0 = q0 + s * SQ
        ws = pl.multiple_of(jnp.maximum(qs0 + SQ - WIN, 0), SQ)
        win_start.append(ws)
        bias_ref[s] = jnp.where(ws + col < qs0 + q_off, 0.0, -1e30)
        for p in range(pairs):
            u = s * pairs + p
            lanes = slice(p * LANES, (p + 1) * LANES)
            q_st = _stack_heads(q_ref[s * SQ:(s + 1) * SQ, lanes])
            z = _dot_nt(q_st, k_ref[pl.ds(ws, WIN), lanes]) + bias_ref[s]
            log_keep = _neg_softplus(z)
            hi, lo = _split_bf16(log_keep)
            hl_ref[u * rows:(u + 1) * rows, 0:WIN] = hi
            hl_ref[u * rows:(u + 1) * rows, WIN:2 * WIN] = lo
            zl_ref[u * rows:(u + 1) * rows, :] = z + log_keep
    s_ref[...] = _dot(hl_ref[...], tri_ref[...])
    for s in range(n_sub):
        ws = win_start[s]
        for p in range(pairs):
            u = s * pairs + p
            lanes = slice(p * LANES, (p + 1) * LANES)
            later = s_ref[u * rows:(u + 1) * rows, :]
            a = jnp.exp(zl_ref[u * rows:(u + 1) * rows, :] + later).astype(BF16)
            o = _dot(a, v_ref[pl.ds(ws, WIN), lanes])
            o_ref[s * SQ:(s + 1) * SQ, lanes] = _unstack_heads(o).astype(BF16)
            left = jnp.max(later[:, 0:LANES], axis=0, keepdims=True)
            left = jnp.max(jnp.where(col[0:1, 0:LANES] == 0, left, -jnp.inf))
            keys_remain = q0 + s * SQ + SQ - WIN > 0
            top = jnp.maximum(top, jnp.where(keys_remain, left, -jnp.inf))

    @pl.when(top > LOG_STICK_FLOOR)
    def _():
        tri = jnp.concatenate([tri_ref[0:TK, 0:TK], tri_ref[TK:2 * TK, 0:TK]], axis=1)
        n_blocks = (q0 + STEP_Q) // TK
        q_pos = q0 + lax.broadcasted_iota(jnp.int32, (2 * STEP_Q, TK), 0) % STEP_Q
        k_off = lax.broadcasted_iota(jnp.int32, (2 * STEP_Q, TK), 1)
        for p in range(pairs):
            lanes = slice(p * LANES, (p + 1) * LANES)
            q_st = _stack_heads(q_ref[:, lanes])
            acc_ref[...] = jnp.zeros_like(acc_ref)
            stick_ref[...] = jnp.zeros_like(stick_ref)

            def cond(carry):
                j, live = carry
                return jnp.logical_and(j < n_blocks, live)

            def body(carry):
                j, _ = carry
                ks = pl.multiple_of(q0 + STEP_Q - (j + 1) * TK, TK)
                z = _dot_nt(q_st, k_ref[pl.ds(ks, TK), lanes])
                z = jnp.where(ks + k_off < q_pos, z, -1e30)
                log_keep = _neg_softplus(z)
                hi, lo = _split_bf16(log_keep)
                sums = _dot(hi, tri) + _dot(lo, tri)
                stick = stick_ref[...]
                a = jnp.exp(z + log_keep + sums[:, :TK] + stick)
                acc_ref[...] += _dot(a.astype(BF16), v_ref[pl.ds(ks, TK), lanes])
                stick = stick + sums[:, TK:]
                stick_ref[...] = stick
                return j + 1, jnp.max(stick) > LOG_STICK_FLOOR

            lax.while_loop(cond, body, (jnp.int32(0), jnp.bool_(True)))
            o_ref[:, lanes] = _unstack_heads(acc_ref[...]).astype(BF16)


def _sb_attention(qkv, bsz, seq, sbw):
    n = qkv.shape[0]
    pairs = sbw // LANES
    steps = seq // STEP_Q
    units = (STEP_Q // SQ) * pairs
    r = lax.broadcasted_iota(jnp.int32, (2 * WIN, WIN), 0) % WIN
    c = lax.broadcasted_iota(jnp.int32, (2 * WIN, WIN), 1)
    tri = jnp.where(r > c, 1.0, 0.0).astype(BF16)
    return pl.pallas_call(
        functools.partial(_sb_attn_kernel, pairs=pairs),
        grid=(bsz, steps),
        in_specs=[
            pl.BlockSpec((STEP_Q, sbw), lambda b, i: (b * steps + i, 0)),
            pl.BlockSpec((seq, sbw), lambda b, i: (b, 1)),
            pl.BlockSpec((seq, sbw), lambda b, i: (b, 2)),
            _const_spec((2 * WIN, WIN)),
        ],
        out_specs=pl.BlockSpec((STEP_Q, sbw), lambda b, i: (b * steps + i, 0)),
        out_shape=jax.ShapeDtypeStruct((n, sbw), BF16),
        scratch_shapes=[
            pltpu.VMEM((units * 2 * SQ, 2 * WIN), BF16),
            pltpu.VMEM((units * 2 * SQ, WIN), F32),
            pltpu.VMEM((units * 2 * SQ, WIN), F32),
            pltpu.VMEM((STEP_Q // SQ, 2 * SQ, WIN), F32),
            pltpu.VMEM((2 * STEP_Q, LANES), F32),
            pltpu.VMEM((2 * STEP_Q, LANES), F32),
        ],
        compiler_params=_params(2),
        name="sb_attn",
    )(qkv, qkv, qkv, tri)


RUN = TM // SUBLANES
PITCH = RUN + SUBLANES
STAGE_BASE = CONV_HALO + SUBLANES
STAGE_ROWS = STAGE_BASE + (SUBLANES - 1) * PITCH + RUN
CONV_GROUPS = (CONV_K - 1) + RUN
CONV_BLOCK = 16
MIX_NORM_PARTS = 2


def _mix_out_kernel(u_ref, uh_ref, a_ref, x_ref, cw_ref, cb_ref, cg_ref, cbeta_ref, unperm_ref,
                    wo_ref, g_ref, b_ref, o_ref, stage_ref, il_ref, y_ref, h_ref,
                    *, tiles, tiles_per_seq, alpha, cch):
    _init_previous_tile(h_ref)
    at_start = (jnp.minimum(pl.program_id(0), tiles - 1) % tiles_per_seq) == 0
    hist = CONV_K - 1
    for s in range(cch // LANES):
        lanes = slice(s * LANES, (s + 1) * LANES)
        stage_ref[s, 0:CONV_HALO, :] = jnp.where(at_start, 0.0, uh_ref[:, lanes])
        for j in range(SUBLANES):
            r0 = STAGE_BASE + j * PITCH
            stage_ref[s, r0:r0 + RUN, :] = u_ref[j * RUN:(j + 1) * RUN, lanes]
        for g in range(CONV_GROUPS):
            off = g - hist
            start = STAGE_BASE + off - (0 if off >= 0 else PITCH - RUN)
            il_ref[s, g * SUBLANES:(g + 1) * SUBLANES, :] = stage_ref[
                s, pl.ds(start, SUBLANES, stride=PITCH), :]

    blocks = RUN // CONV_BLOCK

    def conv_block(idx, carry):
        s = idx // blocks
        base = pl.multiple_of((idx % blocks) * (CONV_BLOCK * SUBLANES), CONV_BLOCK * SUBLANES)
        taps = [jnp.broadcast_to(cw_ref[s, k:k + 1, :], (SUBLANES, LANES)) for k in range(CONV_K)]
        ys = [jnp.broadcast_to(cb_ref[s], (SUBLANES, LANES))] * CONV_BLOCK
        for g in range(CONV_BLOCK + hist):
            v = il_ref[s, pl.ds(base + g * SUBLANES, SUBLANES), :]
            for i in range(CONV_BLOCK):
                if 0 <= g - i < CONV_K:
                    ys[i] = ys[i] + taps[g - i] * v
        for i in range(CONV_BLOCK):
            y_ref[s, pl.ds(base + i * SUBLANES, SUBLANES), :] = ys[i]
        return carry

    lax.fori_loop(0, (cch // LANES) * blocks, conv_block, 0)
    slot, zeros = _norm_previous_tile(h_ref, g_ref, b_ref, o_ref, MIX_NORM_PARTS)
    y = jnp.concatenate([y_ref[s] for s in range(cch // LANES)], axis=1)
    yn = _layer_norm(y, cg_ref[...], cbeta_ref[...])
    act = (yn * jax.nn.sigmoid(yn)).astype(BF16)
    act = _dot(unperm_ref[...], act).astype(BF16)
    part_rows = TM // MIX_NORM_PARTS
    for k in range(MIX_NORM_PARTS):
        rows = slice(k * part_rows, (k + 1) * part_rows)
        mix = _dot(act[rows], wo_ref[0:cch, :]) + _dot(a_ref[rows, :], wo_ref[cch:, :])
        h_ref[slot, rows, :] = (alpha + zeros[k]) * x_ref[rows, :] + mix


def _lane_slabs(t):
    depth, r, c = t.shape
    return t.reshape(depth, r, c // LANES, LANES).transpose(0, 2, 1, 3)


def _unpermute_matrix():
    t = lax.broadcasted_iota(jnp.int32, (TM, TM), 0)
    c = lax.broadcasted_iota(jnp.int32, (TM, TM), 1)
    return jnp.where(c == SUBLANES * (t % RUN) + t // RUN, 1.0, 0.0).astype(BF16)


def _mix_out(u, a, x2, layer, cw_slabs, cb_slabs, cg, cbeta, unperm, w_out, g, b, seq, alpha):
    n, d = x2.shape
    cch = u.shape[1]
    tiles_per_seq = seq // TM
    halo_blocks = TM // CONV_HALO
    slabs = cch // LANES
    tiles = n // TM
    cur = lambda i: jnp.minimum(i, tiles - 1)
    return pl.pallas_call(
        functools.partial(_mix_out_kernel, tiles=tiles, tiles_per_seq=tiles_per_seq, alpha=alpha,
                          cch=cch),
        grid=(tiles + 1,),
        in_specs=[
            pl.BlockSpec((TM, cch), lambda i: (cur(i), 0)),
            pl.BlockSpec((CONV_HALO, cch),
                         lambda i: (jnp.maximum(cur(i) * halo_blocks - 1, 0), 0)),
            pl.BlockSpec((TM, a.shape[1]), lambda i: (cur(i), 0)),
            pl.BlockSpec((TM, d), lambda i: (cur(i), 0)),
            _layer_spec(cw_slabs, layer), _layer_spec(cb_slabs, layer),
            _layer_spec(cg, layer), _layer_spec(cbeta, layer),
            _const_spec((TM, TM)),
            _layer_spec(w_out, layer),
            _layer_spec(g, layer), _layer_spec(b, layer),
        ],
        out_specs=pl.BlockSpec((TM, d), lambda i: (jnp.maximum(i - 1, 0), 0)),
        out_shape=jax.ShapeDtypeStruct((n, d), F32),
        scratch_shapes=[
            pltpu.VMEM((slabs, STAGE_ROWS, LANES), F32),
            pltpu.VMEM((slabs, CONV_GROUPS * SUBLANES, LANES), F32),
            pltpu.VMEM((slabs, TM, LANES), F32),
            pltpu.VMEM((2, TM, d), F32),
        ],
        compiler_params=_params(1),
        name="mix_out",
    )(u, u, a, x2, cw_slabs, cb_slabs, cg, cbeta, unperm, w_out, g, b)


def _init_previous_tile(h_ref):
    @pl.when(pl.program_id(0) == 0)
    def _():
        h_ref[1] = jnp.zeros(h_ref.shape[1:], h_ref.dtype)


def _norm_previous_tile(h_ref, g_ref, b_ref, o_ref, parts):
    i = pl.program_id(0)
    rows = h_ref.shape[1] // parts
    zeros = []
    for k in range(parts):
        out = _layer_norm(h_ref[(i + 1) % 2, k * rows:(k + 1) * rows, :], g_ref[...], b_ref[...])
        o_ref[k * rows:(k + 1) * rows, :] = out
        zeros.append(jnp.sum(out, keepdims=True) * 0.0)
    return i % 2, zeros


def _cross_kernel(x_ref, wq_ref, k_ref, v_ref, wo_ref, g_ref, b_ref, o_ref, *, alpha, dh):
    x = x_ref[...]
    q = (_dot(x.astype(BF16), wq_ref[...]) * (dh ** -0.5)).astype(BF16)
    outs = []
    for h in range(MEM_HEADS):
        sl = slice(h * dh, (h + 1) * dh)
        s = _dot_nt(q[:, sl], k_ref[:, sl])
        e = jnp.exp(s - jnp.max(s, axis=-1, keepdims=True))
        p = e / jnp.sum(e, axis=-1, keepdims=True)
        outs.append(_dot(p.astype(BF16), v_ref[:, sl]).astype(BF16))
    cross = _dot(jnp.concatenate(outs, axis=-1), wo_ref[...])
    o_ref[...] = _layer_norm(alpha * x + cross, g_ref[...], b_ref[...])


def _cross(x2, layer, wq, k, v, wo, g, b, seq, mem_len, alpha):
    n, d = x2.shape
    tiles_per_seq = seq // TM
    return pl.pallas_call(
        functools.partial(_cross_kernel, alpha=alpha, dh=d // MEM_HEADS),
        grid=(n // TM,),
        in_specs=[
            pl.BlockSpec((TM, d), lambda i: (i, 0)),
            _layer_spec(wq, layer),
            pl.BlockSpec((None, mem_len, d), lambda i: (layer, i // tiles_per_seq, 0)),
            pl.BlockSpec((None, mem_len, d), lambda i: (layer, i // tiles_per_seq, 0)),
            _layer_spec(wo, layer),
            _layer_spec(g, layer), _layer_spec(b, layer),
        ],
        out_specs=pl.BlockSpec((TM, d), lambda i: (i, 0)),
        out_shape=jax.ShapeDtypeStruct((n, d), F32),
        compiler_params=_params(1),
        name="cross_attn",
    )(x2, wq, k, v, wo, g, b)


def _ffn_kernel(x_ref, xh_ref, wup_ref, cw_ref, cb_ref, wdn_ref, g_ref, b_ref, o_ref,
                xe_ref, up_ref, acc_ref, h_ref, *, tiles, tiles_per_seq, alpha, n_chunks):
    _init_previous_tile(h_ref)
    slot, zeros = _norm_previous_tile(h_ref, g_ref, b_ref, o_ref, FFN_NORM_PARTS)
    at_start = (jnp.minimum(pl.program_id(0), tiles - 1) % tiles_per_seq) == 0
    xe_ref[0:FFN_HALO] = jnp.where(at_start, 0.0, xh_ref[...]).astype(BF16)
    xe_ref[FFN_HALO:FFN_HALO + TM] = x_ref[...].astype(BF16)

    d_ff = n_chunks * FC
    val_cols = lambda c: slice(c * FC, (c + 1) * FC)
    gate_cols = lambda c: slice(d_ff + c * FC, d_ff + (c + 1) * FC)

    def project_up(c):
        xe = xe_ref[...]
        up_ref[c % 2, 0] = _dot(xe, wup_ref[:, val_cols(c)])
        up_ref[c % 2, 1] = _dot(xe, wup_ref[:, gate_cols(c)])

    def conv(e_ref, cols, bias):
        base = FFN_HALO - (FFN_K - 1)
        y = bias
        for k in range(FFN_K):
            y = y + cw_ref[k:k + 1, cols] * e_ref[base + k:base + k + TM, :]
        return y

    def hidden(c):
        bias = cb_ref[:, val_cols(c)]
        if c < FFN_NORM_PARTS:
            bias = bias + zeros[c]
        val = conv(up_ref.at[c % 2, 0], val_cols(c), bias)
        gate = conv(up_ref.at[c % 2, 1], gate_cols(c), cb_ref[:, gate_cols(c)])
        return ((gate * jax.nn.sigmoid(gate)) * val).astype(BF16)

    project_up(0)
    hids = []
    for c in range(n_chunks):
        if c + 1 < n_chunks:
            project_up(c + 1)
        hids.append(hidden(c))
        if len(hids) == 2 or c + 1 == n_chunks:
            first = c + 1 - len(hids)
            hid = hids[0] if len(hids) == 1 else jnp.concatenate(hids, axis=1)
            part = _dot(hid, wdn_ref[first * FC:(c + 1) * FC, :])
            if first == 0:
                acc_ref[...] = part
            else:
                acc_ref[...] += part
            hids = []
    h_ref[slot] = alpha * x_ref[...] + acc_ref[...]


def _ffn(x2, layer, wup, cw, cb, wdn, g, b, seq, alpha):
    n, d = x2.shape
    n_chunks = wdn.shape[1] // FC
    tiles_per_seq = seq // TM
    halo_blocks = TM // FFN_HALO
    tiles = n // TM
    cur = lambda i: jnp.minimum(i, tiles - 1)
    return pl.pallas_call(
        functools.partial(_ffn_kernel, tiles=tiles, tiles_per_seq=tiles_per_seq, alpha=alpha,
                          n_chunks=n_chunks),
        grid=(tiles + 1,),
        in_specs=[
            pl.BlockSpec((TM, d), lambda i: (cur(i), 0)),
            pl.BlockSpec((FFN_HALO, d), lambda i: (jnp.maximum(cur(i) * halo_blocks - 1, 0), 0)),
            _layer_spec(wup, layer),
            _layer_spec(cw, layer),
            _layer_spec(cb, layer),
            _layer_spec(wdn, layer),
            _layer_spec(g, layer), _layer_spec(b, layer),
        ],
        out_specs=pl.BlockSpec((TM, d), lambda i: (jnp.maximum(i - 1, 0), 0)),
        out_shape=jax.ShapeDtypeStruct((n, d), F32),
        scratch_shapes=[
            pltpu.VMEM((FFN_HALO + TM, d), BF16),
            pltpu.VMEM((2, 2, FFN_HALO + TM, FC), F32),
            pltpu.VMEM((TM, d), F32),
            pltpu.VMEM((2, TM, d), F32),
        ],
        compiler_params=_params(1),
        name="conv_ffn",
    )(x2, x2, wup, cw, cb, wdn, g, b)


def kernel(x, mem, w_in, conv_w, conv_b, conv_ln_g, conv_ln_b, w_out, ln1_g, ln1_b,
           mem_wq, mem_wk, mem_wv, mem_wo, ln2_g, ln2_b,
           ffn_up, ffn_conv_w, ffn_conv_b, ffn_down, ln3_g, ln3_b):
    bsz, seq, d = x.shape
    depth = w_in.shape[0]
    mem_len = mem.shape[1]
    cch = conv_w.shape[2]
    sbw = SB_HEADS * SB_HEAD_DIM
    d_ff = ffn_down.shape[1]
    alpha = (2.0 * depth) ** 0.25
    assert seq % TM == 0 and seq % STEP_Q == 0 and d_ff % FC == 0
    assert w_in.shape[2] == 2 * cch + 3 * sbw and sbw % LANES == 0

    w_in_b = w_in.astype(BF16)
    w_out_b = w_out.astype(BF16)
    wq_b = mem_wq.astype(BF16)
    wo_b = mem_wo.astype(BF16)
    wup_b = ffn_up.astype(BF16)
    wdn_b = ffn_down.astype(BF16)
    cw_slabs = _lane_slabs(conv_w)
    cb_slabs = _lane_slabs(_rows(conv_b))
    unperm = _unpermute_matrix()

    mem_k, mem_v = _memkv(mem.reshape(bsz * mem_len, d).astype(BF16),
                          mem_wk.astype(BF16), mem_wv.astype(BF16), mem_len)

    h = x.reshape(bsz * seq, d)
    for l in range(depth):
        u, qkv = _inproj(h, w_in_b, l, cch, sbw)
        a = _sb_attention(qkv, bsz, seq, sbw)
        h = _mix_out(u, a, h, l, cw_slabs, cb_slabs, _rows(conv_ln_g), _rows(conv_ln_b), unperm,
                     w_out_b, _rows(ln1_g), _rows(ln1_b), seq, alpha)
        h = _cross(h, l, wq_b, mem_k, mem_v, wo_b, _rows(ln2_g), _rows(ln2_b),
                   seq, mem_len, alpha)
        h = _ffn(h, l, wup_b, ffn_conv_w, _rows(ffn_conv_b), wdn_b, _rows(ln3_g), _rows(ln3_b),
                 seq, alpha)
    return h.reshape(bsz, seq, d)
```

```python
import functools

import jax
import jax.numpy as jnp
from jax import lax
from jax.experimental import pallas as pl
from jax.experimental.pallas import tpu as pltpu

F32 = jnp.float32
BF16 = jnp.bfloat16

CONV_K = 31
SB_HEADS = 8
SB_HEAD_DIM = 64
MEM_HEADS = 4
FFN_K = 3
LN_EPS = 1e-5

LANES = 128
SUBLANES = 8
VMEM_LIMIT = 56 * 1024 * 1024

TM = 512
TM_IN = 1024
CROSS_PARTS = 4
STEP_Q = 512
SQ = 64
WIN = 256
TK = 128
FC = 256
CONV_HALO = 32
FFN_HALO = 8
FFN_NORM_PARTS = 8
LOG_STICK_FLOOR = -100.0


def _dot(a, b):
    return jnp.dot(a, b, preferred_element_type=F32)


def _dot_nt(a, b):
    return lax.dot_general(a, b, (((1,), (1,)), ((), ())), preferred_element_type=F32)


def _layer_norm(h, g, b):
    mu = jnp.mean(h, axis=-1, keepdims=True)
    d = h - mu
    var = jnp.mean(d * d, axis=-1, keepdims=True)
    return d * lax.rsqrt(var + LN_EPS) * g + b


def _params(n_axes):
    return pltpu.CompilerParams(
        dimension_semantics=("arbitrary",) * n_axes, vmem_limit_bytes=VMEM_LIMIT)


def _const_spec(shape):
    nd = len(shape)
    return pl.BlockSpec(shape, lambda *_: (0,) * nd, pipeline_mode=pl.Buffered(1))


def _layer_spec(stacked, layer):
    tail = stacked.shape[1:]
    return pl.BlockSpec((None,) + tail, lambda *_: (layer,) + (0,) * len(tail),
                        pipeline_mode=pl.Buffered(1))


def _rows(stacked):
    return stacked[:, None, :]


def _memkv_kernel(mem_ref, wk_ref, wv_ref, k_ref, v_ref):
    m = mem_ref[...]
    k_ref[0] = _dot(m, wk_ref[0]).astype(BF16)
    v_ref[0] = _dot(m, wv_ref[0]).astype(BF16)


def _memkv(mem2, wk, wv):
    depth, d, _ = wk.shape
    rows = mem2.shape[0]
    return pl.pallas_call(
        _memkv_kernel,
        grid=(depth,),
        in_specs=[
            _const_spec((rows, d)),
            pl.BlockSpec((1, d, d), lambda l: (l, 0, 0)),
            pl.BlockSpec((1, d, d), lambda l: (l, 0, 0)),
        ],
        out_specs=[
            pl.BlockSpec((1, rows, d), lambda l: (l, 0, 0)),
            pl.BlockSpec((1, rows, d), lambda l: (l, 0, 0)),
        ],
        out_shape=[jax.ShapeDtypeStruct((depth, rows, d), BF16)] * 2,
        compiler_params=_params(1),
        name="mem_kv",
    )(mem2, wk, wv)


def _inproj_kernel(x_ref, w_ref, u_ref, qkv_ref, *, cch, sbw):
    xb = x_ref[...].astype(BF16)
    ga = _dot(xb, w_ref[:, 0:cch])
    gg = _dot(xb, w_ref[:, cch:2 * cch])
    u_ref[...] = ga * jax.nn.sigmoid(gg)
    q = _dot(xb, w_ref[:, 2 * cch:2 * cch + sbw]) * (SB_HEAD_DIM ** -0.5)
    qkv_ref[:, 0:sbw] = q.astype(BF16)
    qkv_ref[:, sbw:3 * sbw] = _dot(xb, w_ref[:, 2 * cch + sbw:]).astype(BF16)


def _inproj(x2, w_in, layer, cch, sbw):
    n, d = x2.shape
    return pl.pallas_call(
        functools.partial(_inproj_kernel, cch=cch, sbw=sbw),
        grid=(n // TM_IN,),
        in_specs=[
            pl.BlockSpec((TM_IN, d), lambda i: (i, 0)),
            _layer_spec(w_in, layer),
        ],
        out_specs=[
            pl.BlockSpec((TM_IN, cch), lambda i: (i, 0)),
            pl.BlockSpec((TM_IN, 3 * sbw), lambda i: (i, 0)),
        ],
        out_shape=[
            jax.ShapeDtypeStruct((n, cch), F32),
            jax.ShapeDtypeStruct((n, 3 * sbw), BF16),
        ],
        compiler_params=_params(1),
        name="in_proj",
    )(x2, w_in)


def _neg_softplus(z):
    return -(jnp.maximum(z, 0.0) + jnp.log(1.0 + jnp.exp(-jnp.abs(z))))


def _split_bf16(t):
    hi = t.astype(BF16)
    return hi, (t - hi.astype(F32)).astype(BF16)


def _stack_heads(q_pair):
    lane = lax.broadcasted_iota(jnp.int32, q_pair.shape, 1)
    zero = jnp.zeros_like(q_pair)
    return jnp.concatenate([jnp.where(lane < SB_HEAD_DIM, q_pair, zero),
                            jnp.where(lane < SB_HEAD_DIM, zero, q_pair)], axis=0)


def _unstack_heads(o_stacked):
    rows = o_stacked.shape[0] // 2
    lane = lax.broadcasted_iota(jnp.int32, (rows, LANES), 1)
    return jnp.where(lane < SB_HEAD_DIM, o_stacked[:rows], o_stacked[rows:])


def _sb_attn_kernel(q_ref, k_ref, v_ref, tri_ref, o_ref,
                    hl_ref, zl_ref, s_ref, bias_ref, acc_ref, stick_ref, *, pairs):
    q0 = pl.program_id(1) * STEP_Q
    n_sub = STEP_Q // SQ
    rows = 2 * SQ
    col = lax.broadcasted_iota(jnp.int32, (rows, WIN), 1)
    q_off = lax.broadcasted_iota(jnp.int32, (rows, WIN), 0) % SQ

    win_start = []
    top = jnp.float32(-jnp.inf)
    for s in range(n_sub):
        qs0 = q0 + s * SQ
        ws = pl.multiple_of(jnp.maximum(qs0 + SQ - WIN, 0), SQ)
        win_start.append(ws)
        bias_ref[s] = jnp.where(ws + col < qs0 + q_off, 0.0, -1e30)
        for p in range(pairs):
            u = s * pairs + p
            lanes = slice(p * LANES, (p + 1) * LANES)
            q_st = _stack_heads(q_ref[s * SQ:(s + 1) * SQ, lanes])
            z = _dot_nt(q_st, k_ref[pl.ds(ws, WIN), lanes]) + bias_ref[s]
            log_keep = _neg_softplus(z)
            hi, lo = _split_bf16(log_keep)
            hl_ref[u * rows:(u + 1) * rows, 0:WIN] = hi
            hl_ref[u * rows:(u + 1) * rows, WIN:2 * WIN] = lo
            zl_ref[u * rows:(u + 1) * rows, :] = z + log_keep
    s_ref[...] = _dot(hl_ref[...], tri_ref[...])
    for s in range(n_sub):
        ws = win_start[s]
        for p in range(pairs):
            u = s * pairs + p
            lanes = slice(p * LANES, (p + 1) * LANES)
            later = s_ref[u * rows:(u + 1) * rows, :]
            a = jnp.exp(zl_ref[u * rows:(u + 1) * rows, :] + later).astype(BF16)
            o = _dot(a, v_ref[pl.ds(ws, WIN), lanes])
            o_ref[s * SQ:(s + 1) * SQ, lanes] = _unstack_heads(o).astype(BF16)
            left = jnp.max(later[:, 0:LANES], axis=0, keepdims=True)
            left = jnp.max(jnp.where(col[0:1, 0:LANES] == 0, left, -jnp.inf))
            keys_remain = q0 + s * SQ + SQ - WIN > 0
            top = jnp.maximum(top, jnp.where(keys_remain, left, -jnp.inf))

    @pl.when(top > LOG_STICK_FLOOR)
    def _():
        tri = jnp.concatenate([tri_ref[0:TK, 0:TK], tri_ref[TK:2 * TK, 0:TK]], axis=1)
        n_blocks = (q0 + STEP_Q) // TK
        q_pos = q0 + lax.broadcasted_iota(jnp.int32, (2 * STEP_Q, TK), 0) % STEP_Q
        k_off = lax.broadcasted_iota(jnp.int32, (2 * STEP_Q, TK), 1)
        for p in range(pairs):
            lanes = slice(p * LANES, (p + 1) * LANES)
            q_st = _stack_heads(q_ref[:, lanes])
            acc_ref[...] = jnp.zeros_like(acc_ref)
            stick_ref[...] = jnp.zeros_like(stick_ref)

            def cond(carry):
                j, live = carry
                return jnp.logical_and(j < n_blocks, live)

            def body(carry):
                j, _ = carry
                ks = pl.multiple_of(q0 + STEP_Q - (j + 1) * TK, TK)
                z = _dot_nt(q_st, k_ref[pl.ds(ks, TK), lanes])
                z = jnp.where(ks + k_off < q_pos, z, -1e30)
                log_keep = _neg_softplus(z)
                hi, lo = _split_bf16(log_keep)
                sums = _dot(hi, tri) + _dot(lo, tri)
                stick = stick_ref[...]
                a = jnp.exp(z + log_keep + sums[:, :TK] + stick)
                acc_ref[...] += _dot(a.astype(BF16), v_ref[pl.ds(ks, TK), lanes])
                stick = stick + sums[:, TK:]
                stick_ref[...] = stick
                return j + 1, jnp.max(stick) > LOG_STICK_FLOOR

            lax.while_loop(cond, body, (jnp.int32(0), jnp.bool_(True)))
            o_ref[:, lanes] = _unstack_heads(acc_ref[...]).astype(BF16)


def _sb_attention(qkv, bsz, seq, sbw):
    n = qkv.shape[0]
    pairs = sbw // LANES
    steps = seq // STEP_Q
    units = (STEP_Q // SQ) * pairs
    r = lax.broadcasted_iota(jnp.int32, (2 * WIN, WIN), 0) % WIN
    c = lax.broadcasted_iota(jnp.int32, (2 * WIN, WIN), 1)
    tri = jnp.where(r > c, 1.0, 0.0).astype(BF16)
    return pl.pallas_call(
        functools.partial(_sb_attn_kernel, pairs=pairs),
        grid=(bsz, steps),
        in_specs=[
            pl.BlockSpec((STEP_Q, sbw), lambda b, i: (b * steps + i, 0)),
            pl.BlockSpec((seq, sbw), lambda b, i: (b, 1)),
            pl.BlockSpec((seq, sbw), lambda b, i: (b, 2)),
            _const_spec((2 * WIN, WIN)),
        ],
        out_specs=pl.BlockSpec((STEP_Q, sbw), lambda b, i: (b * steps + i, 0)),
        out_shape=jax.ShapeDtypeStruct((n, sbw), BF16),
        scratch_shapes=[
            pltpu.VMEM((units * 2 * SQ, 2 * WIN), BF16),
            pltpu.VMEM((units * 2 * SQ, WIN), F32),
            pltpu.VMEM((units * 2 * SQ, WIN), F32),
            pltpu.VMEM((STEP_Q // SQ, 2 * SQ, WIN), F32),
            pltpu.VMEM((2 * STEP_Q, LANES), F32),
            pltpu.VMEM((2 * STEP_Q, LANES), F32),
        ],
        compiler_params=_params(2),
        name="sb_attn",
    )(qkv, qkv, qkv, tri)


RUN = TM // SUBLANES
PITCH = RUN + SUBLANES
STAGE_BASE = CONV_HALO + SUBLANES
STAGE_ROWS = STAGE_BASE + (SUBLANES - 1) * PITCH + RUN
CONV_GROUPS = (CONV_K - 1) + RUN
CONV_BLOCK = 16
MIX_NORM_PARTS = 2


def _mix_out_kernel(u_ref, uh_ref, a_ref, x_ref, cw_ref, cb_ref, cg_ref, cbeta_ref, unperm_ref,
                    wo_ref, g_ref, b_ref, o_ref, stage_ref, il_ref, y_ref, h_ref,
                    *, tiles, tiles_per_seq, alpha, cch):
    _init_previous_tile(h_ref)
    at_start = (jnp.minimum(pl.program_id(0), tiles - 1) % tiles_per_seq) == 0
    hist = CONV_K - 1
    for s in range(cch // LANES):
        lanes = slice(s * LANES, (s + 1) * LANES)
        stage_ref[s, 0:CONV_HALO, :] = jnp.where(at_start, 0.0, uh_ref[:, lanes])
        for j in range(SUBLANES):
            r0 = STAGE_BASE + j * PITCH
            stage_ref[s, r0:r0 + RUN, :] = u_ref[j * RUN:(j + 1) * RUN, lanes]
        for g in range(CONV_GROUPS):
            off = g - hist
            start = STAGE_BASE + off - (0 if off >= 0 else PITCH - RUN)
            il_ref[s, g * SUBLANES:(g + 1) * SUBLANES, :] = stage_ref[
                s, pl.ds(start, SUBLANES, stride=PITCH), :]

    blocks = RUN // CONV_BLOCK

    def conv_block(idx, carry):
        s = idx // blocks
        base = pl.multiple_of((idx % blocks) * (CONV_BLOCK * SUBLANES), CONV_BLOCK * SUBLANES)
        taps = [jnp.broadcast_to(cw_ref[s, k:k + 1, :], (SUBLANES, LANES)) for k in range(CONV_K)]
        ys = [jnp.broadcast_to(cb_ref[s], (SUBLANES, LANES))] * CONV_BLOCK
        for g in range(CONV_BLOCK + hist):
            v = il_ref[s, pl.ds(base + g * SUBLANES, SUBLANES), :]
            for i in range(CONV_BLOCK):
                if 0 <= g - i < CONV_K:
                    ys[i] = ys[i] + taps[g - i] * v
        for i in range(CONV_BLOCK):
            y_ref[s, pl.ds(base + i * SUBLANES, SUBLANES), :] = ys[i]
        return carry

    lax.fori_loop(0, (cch // LANES) * blocks, conv_block, 0)
    slot, zeros = _norm_previous_tile(h_ref, g_ref, b_ref, o_ref, MIX_NORM_PARTS)
    y = jnp.concatenate([y_ref[s] for s in range(cch // LANES)], axis=1)
    yn = _layer_norm(y, cg_ref[...], cbeta_ref[...])
    act = (yn * jax.nn.sigmoid(yn)).astype(BF16)
    act = _dot(unperm_ref[...], act).astype(BF16)
    part_rows = TM // MIX_NORM_PARTS
    for k in range(MIX_NORM_PARTS):
        rows = slice(k * part_rows, (k + 1) * part_rows)
        mix = _dot(act[rows], wo_ref[0:cch, :]) + _dot(a_ref[rows, :], wo_ref[cch:, :])
        h_ref[slot, rows, :] = (alpha + zeros[k]) * x_ref[rows, :] + mix


def _lane_slabs(t):
    depth, r, c = t.shape
    return t.reshape(depth, r, c // LANES, LANES).transpose(0, 2, 1, 3)


def _unpermute_matrix():
    t = lax.broadcasted_iota(jnp.int32, (TM, TM), 0)
    c = lax.broadcasted_iota(jnp.int32, (TM, TM), 1)
    return jnp.where(c == SUBLANES * (t % RUN) + t // RUN, 1.0, 0.0).astype(BF16)


def _mix_out(u, a, x2, layer, cw_slabs, cb_slabs, cg, cbeta, unperm, w_out, g, b, seq, alpha):
    n, d = x2.shape
    cch = u.shape[1]
    tiles_per_seq = seq // TM
    halo_blocks = TM // CONV_HALO
    slabs = cch // LANES
    tiles = n // TM
    cur = lambda i: jnp.minimum(i, tiles - 1)
    return pl.pallas_call(
        functools.partial(_mix_out_kernel, tiles=tiles, tiles_per_seq=tiles_per_seq, alpha=alpha,
                          cch=cch),
        grid=(tiles + 1,),
        in_specs=[
            pl.BlockSpec((TM, cch), lambda i: (cur(i), 0)),
            pl.BlockSpec((CONV_HALO, cch),
                         lambda i: (jnp.maximum(cur(i) * halo_blocks - 1, 0), 0)),
            pl.BlockSpec((TM, a.shape[1]), lambda i: (cur(i), 0)),
            pl.BlockSpec((TM, d), lambda i: (cur(i), 0)),
            _layer_spec(cw_slabs, layer), _layer_spec(cb_slabs, layer),
            _layer_spec(cg, layer), _layer_spec(cbeta, layer),
            _const_spec((TM, TM)),
            _layer_spec(w_out, layer),
            _layer_spec(g, layer), _layer_spec(b, layer),
        ],
        out_specs=pl.BlockSpec((TM, d), lambda i: (jnp.maximum(i - 1, 0), 0)),
        out_shape=jax.ShapeDtypeStruct((n, d), F32),
        scratch_shapes=[
            pltpu.VMEM((slabs, STAGE_ROWS, LANES), F32),
            pltpu.VMEM((slabs, CONV_GROUPS * SUBLANES, LANES), F32),
            pltpu.VMEM((slabs, TM, LANES), F32),
            pltpu.VMEM((2, TM, d), F32),
        ],
        compiler_params=_params(1),
        name="mix_out",
    )(u, u, a, x2, cw_slabs, cb_slabs, cg, cbeta, unperm, w_out, g, b)


def _init_previous_tile(h_ref):
    @pl.when(pl.program_id(0) == 0)
    def _():
        h_ref[1] = jnp.zeros(h_ref.shape[1:], h_ref.dtype)


def _norm_previous_tile(h_ref, g_ref, b_ref, o_ref, parts):
    i = pl.program_id(0)
    rows = h_ref.shape[1] // parts
    zeros = []
    for k in range(parts):
        out = _layer_norm(h_ref[(i + 1) % 2, k * rows:(k + 1) * rows, :], g_ref[...], b_ref[...])
        o_ref[k * rows:(k + 1) * rows, :] = out
        zeros.append(jnp.sum(out, keepdims=True) * 0.0)
    return i % 2, zeros


def _cross_kernel(x_ref, wq_ref, k_ref, v_ref, wo_ref, g_ref, b_ref, o_ref, *, alpha, dh):
    part_rows = x_ref.shape[0] // CROSS_PARTS
    zero = None
    for k in range(CROSS_PARTS):
        rows = slice(k * part_rows, (k + 1) * part_rows)
        x = x_ref[rows, :]
        q = (_dot(x.astype(BF16), wq_ref[...]) * (dh ** -0.5)).astype(BF16)
        outs = []
        for h in range(MEM_HEADS):
            sl = slice(h * dh, (h + 1) * dh)
            s = _dot_nt(q[:, sl], k_ref[:, sl])
            top = jnp.max(s, axis=-1, keepdims=True)
            if zero is not None and h == MEM_HEADS - 1:
                top = top + zero
            e = jnp.exp(s - top)
            p = e / jnp.sum(e, axis=-1, keepdims=True)
            outs.append(_dot(p.astype(BF16), v_ref[:, sl]).astype(BF16))
        cross = _dot(jnp.concatenate(outs, axis=-1), wo_ref[...])
        out = _layer_norm(alpha * x + cross, g_ref[...], b_ref[...])
        o_ref[rows, :] = out
        zero = jnp.sum(out, keepdims=True) * 0.0


def _cross(x2, layer, wq, k, v, wo, g, b, seq, mem_len, alpha):
    n, d = x2.shape
    tm = TM * CROSS_PARTS
    assert seq % tm == 0
    tiles_per_seq = seq // tm
    return pl.pallas_call(
        functools.partial(_cross_kernel, alpha=alpha, dh=d // MEM_HEADS),
        grid=(n // tm,),
        in_specs=[
            pl.BlockSpec((tm, d), lambda i: (i, 0)),
            _layer_spec(wq, layer),
            pl.BlockSpec((None, mem_len, d), lambda i: (layer, i // tiles_per_seq, 0)),
            pl.BlockSpec((None, mem_len, d), lambda i: (layer, i // tiles_per_seq, 0)),
            _layer_spec(wo, layer),
            _layer_spec(g, layer), _layer_spec(b, layer),
        ],
        out_specs=pl.BlockSpec((tm, d), lambda i: (i, 0)),
        out_shape=jax.ShapeDtypeStruct((n, d), F32),
        compiler_params=_params(1),
        name="cross_attn",
    )(x2, wq, k, v, wo, g, b)


def _ffn_kernel(x_ref, xh_ref, wup_ref, cw_ref, cb_ref, wdn_ref, g_ref, b_ref, o_ref,
                xe_ref, up_ref, acc_ref, h_ref, *, tiles, tiles_per_seq, alpha, n_chunks):
    _init_previous_tile(h_ref)
    slot, zeros = _norm_previous_tile(h_ref, g_ref, b_ref, o_ref, FFN_NORM_PARTS)
    at_start = (jnp.minimum(pl.program_id(0), tiles - 1) % tiles_per_seq) == 0
    xe_ref[0:FFN_HALO] = jnp.where(at_start, 0.0, xh_ref[...]).astype(BF16)
    xe_ref[FFN_HALO:FFN_HALO + TM] = x_ref[...].astype(BF16)

    d_ff = n_chunks * FC
    val_cols = lambda c: slice(c * FC, (c + 1) * FC)
    gate_cols = lambda c: slice(d_ff + c * FC, d_ff + (c + 1) * FC)

    def project_up(c):
        xe = xe_ref[...]
        up_ref[c % 2, 0] = _dot(xe, wup_ref[:, val_cols(c)])
        up_ref[c % 2, 1] = _dot(xe, wup_ref[:, gate_cols(c)])

    def conv(e_ref, cols, bias):
        base = FFN_HALO - (FFN_K - 1)
        y = bias
        for k in range(FFN_K):
            y = y + cw_ref[k:k + 1, cols] * e_ref[base + k:base + k + TM, :]
        return y

    def hidden(c):
        bias = cb_ref[:, val_cols(c)]
        if c < FFN_NORM_PARTS:
            bias = bias + zeros[c]
        val = conv(up_ref.at[c % 2, 0], val_cols(c), bias)
        gate = conv(up_ref.at[c % 2, 1], gate_cols(c), cb_ref[:, gate_cols(c)])
        return ((gate * jax.nn.sigmoid(gate)) * val).astype(BF16)

    project_up(0)
    hids = []
    for c in range(n_chunks):
        if c + 1 < n_chunks:
            project_up(c + 1)
        hids.append(hidden(c))
        if len(hids) == 2 or c + 1 == n_chunks:
            first = c + 1 - len(hids)
            hid = hids[0] if len(hids) == 1 else jnp.concatenate(hids, axis=1)
            part = _dot(hid, wdn_ref[first * FC:(c + 1) * FC, :])
            if first == 0:
                acc_ref[...] = part
            else:
                acc_ref[...] += part
            hids = []
    h_ref[slot] = alpha * x_ref[...] + acc_ref[...]


def _ffn(x2, layer, wup, cw, cb, wdn, g, b, seq, alpha):
    n, d = x2.shape
    n_chunks = wdn.shape[1] // FC
    tiles_per_seq = seq // TM
    halo_blocks = TM // FFN_HALO
    tiles = n // TM
    cur = lambda i: jnp.minimum(i, tiles - 1)
    return pl.pallas_call(
        functools.partial(_ffn_kernel, tiles=tiles, tiles_per_seq=tiles_per_seq, alpha=alpha,
                          n_chunks=n_chunks),
        grid=(tiles + 1,),
        in_specs=[
            pl.BlockSpec((TM, d), lambda i: (cur(i), 0)),
            pl.BlockSpec((FFN_HALO, d), lambda i: (jnp.maximum(cur(i) * halo_blocks - 1, 0), 0)),
            _layer_spec(wup, layer),
            _layer_spec(cw, layer),
            _layer_spec(cb, layer),
            _layer_spec(wdn, layer),
            _layer_spec(g, layer), _layer_spec(b, layer),
        ],
        out_specs=pl.BlockSpec((TM, d), lambda i: (jnp.maximum(i - 1, 0), 0)),
        out_shape=jax.ShapeDtypeStruct((n, d), F32),
        scratch_shapes=[
            pltpu.VMEM((FFN_HALO + TM, d), BF16),
            pltpu.VMEM((2, 2, FFN_HALO + TM, FC), F32),
            pltpu.VMEM((TM, d), F32),
            pltpu.VMEM((2, TM, d), F32),
        ],
        compiler_params=_params(1),
        name="conv_ffn",
    )(x2, x2, wup, cw, cb, wdn, g, b)


def kernel(x, mem, w_in, conv_w, conv_b, conv_ln_g, conv_ln_b, w_out, ln1_g, ln1_b,
           mem_wq, mem_wk, mem_wv, mem_wo, ln2_g, ln2_b,
           ffn_up, ffn_conv_w, ffn_conv_b, ffn_down, ln3_g, ln3_b):
    bsz, seq, d = x.shape
    depth = w_in.shape[0]
    mem_len = mem.shape[1]
    cch = conv_w.shape[2]
    sbw = SB_HEADS * SB_HEAD_DIM
    d_ff = ffn_down.shape[1]
    alpha = (2.0 * depth) ** 0.25
    assert seq % TM == 0 and seq % STEP_Q == 0 and d_ff % FC == 0
    assert w_in.shape[2] == 2 * cch + 3 * sbw and sbw % LANES == 0

    w_in_b = w_in.astype(BF16)
    w_out_b = w_out.astype(BF16)
    wq_b = mem_wq.astype(BF16)
    wo_b = mem_wo.astype(BF16)
    wup_b = ffn_up.astype(BF16)
    wdn_b = ffn_down.astype(BF16)
    cw_slabs = _lane_slabs(conv_w)
    cb_slabs = _lane_slabs(_rows(conv_b))
    unperm = _unpermute_matrix()

    mem_k, mem_v = _memkv(mem.reshape(bsz * mem_len, d).astype(BF16),
                          mem_wk.astype(BF16), mem_wv.astype(BF16))

    h = x.reshape(bsz * seq, d)
    for l in range(depth):
        u, qkv = _inproj(h, w_in_b, l, cch, sbw)
        a = _sb_attention(qkv, bsz, seq, sbw)
        h = _mix_out(u, a, h, l, cw_slabs, cb_slabs, _rows(conv_ln_g), _rows(conv_ln_b), unperm,
                     w_out_b, _rows(ln1_g), _rows(ln1_b), seq, alpha)
        h = _cross(h, l, wq_b, mem_k, mem_v, wo_b, _rows(ln2_g), _rows(ln2_b),
                   seq, mem_len, alpha)
        h = _ffn(h, l, wup_b, ffn_conv_w, _rows(ffn_conv_b), wdn_b, _rows(ln3_g), _rows(ln3_b),
                 seq, alpha)
    return h.reshape(bsz, seq, d)
```

```python
import functools

import jax
import jax.numpy as jnp
from jax import lax
from jax.experimental import pallas as pl
from jax.experimental.pallas import tpu as pltpu

F32 = jnp.float32
BF16 = jnp.bfloat16

CONV_K = 31
SB_HEADS = 8
SB_HEAD_DIM = 64
MEM_HEADS = 4
FFN_K = 3
LN_EPS = 1e-5

LANES = 128
SUBLANES = 8
VMEM_LIMIT = 56 * 1024 * 1024

TM = 512
TM_IN = 1024
CROSS_PARTS = 4
STEP_Q = 1024
SQ = 64
WIN = 256
TK = 128
FC = 256
CONV_HALO = 32
FFN_HALO = 8
FFN_NORM_PARTS = 8
LOG_STICK_FLOOR = -100.0


def _dot(a, b):
    return jnp.dot(a, b, preferred_element_type=F32)


def _dot_nt(a, b):
    return lax.dot_general(a, b, (((1,), (1,)), ((), ())), preferred_element_type=F32)


def _layer_norm(h, g, b):
    mu = jnp.mean(h, axis=-1, keepdims=True)
    d = h - mu
    var = jnp.mean(d * d, axis=-1, keepdims=True)
    return d * lax.rsqrt(var + LN_EPS) * g + b


def _params(n_axes):
    return pltpu.CompilerParams(
        dimension_semantics=("arbitrary",) * n_axes, vmem_limit_bytes=VMEM_LIMIT)


def _const_spec(shape):
    nd = len(shape)
    return pl.BlockSpec(shape, lambda *_: (0,) * nd, pipeline_mode=pl.Buffered(1))


def _layer_spec(stacked, layer):
    tail = stacked.shape[1:]
    return pl.BlockSpec((None,) + tail, lambda *_: (layer,) + (0,) * len(tail),
                        pipeline_mode=pl.Buffered(1))


def _rows(stacked):
    return stacked[:, None, :]


def _memkv_kernel(mem_ref, wk_ref, wv_ref, k_ref, v_ref):
    m = mem_ref[...]
    k_ref[0] = _dot(m, wk_ref[0]).astype(BF16)
    v_ref[0] = _dot(m, wv_ref[0]).astype(BF16)


def _memkv(mem2, wk, wv):
    depth, d, _ = wk.shape
    rows = mem2.shape[0]
    return pl.pallas_call(
        _memkv_kernel,
        grid=(depth,),
        in_specs=[
            _const_spec((rows, d)),
            pl.BlockSpec((1, d, d), lambda l: (l, 0, 0)),
            pl.BlockSpec((1, d, d), lambda l: (l, 0, 0)),
        ],
        out_specs=[
            pl.BlockSpec((1, rows, d), lambda l: (l, 0, 0)),
            pl.BlockSpec((1, rows, d), lambda l: (l, 0, 0)),
        ],
        out_shape=[jax.ShapeDtypeStruct((depth, rows, d), BF16)] * 2,
        compiler_params=_params(1),
        name="mem_kv",
    )(mem2, wk, wv)


def _inproj_kernel(x_ref, w_ref, u_ref, qkv_ref, *, cch, sbw):
    xb = x_ref[...].astype(BF16)
    ga = _dot(xb, w_ref[:, 0:cch])
    gg = _dot(xb, w_ref[:, cch:2 * cch])
    u_ref[...] = ga * jax.nn.sigmoid(gg)
    q = _dot(xb, w_ref[:, 2 * cch:2 * cch + sbw]) * (SB_HEAD_DIM ** -0.5)
    qkv_ref[:, 0:sbw] = q.astype(BF16)
    qkv_ref[:, sbw:3 * sbw] = _dot(xb, w_ref[:, 2 * cch + sbw:]).astype(BF16)


def _inproj(x2, w_in, layer, cch, sbw):
    n, d = x2.shape
    return pl.pallas_call(
        functools.partial(_inproj_kernel, cch=cch, sbw=sbw),
        grid=(n // TM_IN,),
        in_specs=[
            pl.BlockSpec((TM_IN, d), lambda i: (i, 0)),
            _layer_spec(w_in, layer),
        ],
        out_specs=[
            pl.BlockSpec((TM_IN, cch), lambda i: (i, 0)),
            pl.BlockSpec((TM_IN, 3 * sbw), lambda i: (i, 0)),
        ],
        out_shape=[
            jax.ShapeDtypeStruct((n, cch), F32),
            jax.ShapeDtypeStruct((n, 3 * sbw), BF16),
        ],
        compiler_params=_params(1),
        name="in_proj",
    )(x2, w_in)


def _neg_softplus(z):
    return -(jnp.maximum(z, 0.0) + jnp.log(1.0 + jnp.exp(-jnp.abs(z))))


def _split_bf16(t):
    hi = t.astype(BF16)
    return hi, (t - hi.astype(F32)).astype(BF16)


def _stack_heads(q_pair):
    lane = lax.broadcasted_iota(jnp.int32, q_pair.shape, 1)
    zero = jnp.zeros_like(q_pair)
    return jnp.concatenate([jnp.where(lane < SB_HEAD_DIM, q_pair, zero),
                            jnp.where(lane < SB_HEAD_DIM, zero, q_pair)], axis=0)


def _unstack_heads(o_stacked):
    rows = o_stacked.shape[0] // 2
    lane = lax.broadcasted_iota(jnp.int32, (rows, LANES), 1)
    return jnp.where(lane < SB_HEAD_DIM, o_stacked[:rows], o_stacked[rows:])


def _sb_attn_kernel(q_ref, k_ref, v_ref, tri_ref, o_ref,
                    hl_ref, zl_ref, s_ref, bias_ref, acc_ref, stick_ref, *, pairs):
    q0 = pl.program_id(1) * STEP_Q
    n_sub = STEP_Q // SQ
    rows = 2 * SQ
    col = lax.broadcasted_iota(jnp.int32, (rows, WIN), 1)
    q_off = lax.broadcasted_iota(jnp.int32, (rows, WIN), 0) % SQ

    win_start = []
    top = jnp.float32(-jnp.inf)
    for s in range(n_sub):
        qs0 = q0 + s * SQ
        ws = pl.multiple_of(jnp.maximum(qs0 + SQ - WIN, 0), SQ)
        win_start.append(ws)
        bias_ref[s] = jnp.where(ws + col < qs0 + q_off, 0.0, -1e30)
        for p in range(pairs):
            u = s * pairs + p
            lanes = slice(p * LANES, (p + 1) * LANES)
            q_st = _stack_heads(q_ref[s * SQ:(s + 1) * SQ, lanes])
            z = _dot_nt(q_st, k_ref[pl.ds(ws, WIN), lanes]) + bias_ref[s]
            log_keep = _neg_softplus(z)
            hi, lo = _split_bf16(log_keep)
            hl_ref[u * rows:(u + 1) * rows, 0:WIN] = hi
            hl_ref[u * rows:(u + 1) * rows, WIN:2 * WIN] = lo
            zl_ref[u * rows:(u + 1) * rows, :] = z + log_keep
    s_ref[...] = _dot(hl_ref[...], tri_ref[...])
    for s in range(n_sub):
        ws = win_start[s]
        for p in range(pairs):
            u = s * pairs + p
            lanes = slice(p * LANES, (p + 1) * LANES)
            later = s_ref[u * rows:(u + 1) * rows, :]
            a = jnp.exp(zl_ref[u * rows:(u + 1) * rows, :] + later).astype(BF16)
            o = _dot(a, v_ref[pl.ds(ws, WIN), lanes])
            o_ref[s * SQ:(s + 1) * SQ, lanes] = _unstack_heads(o).astype(BF16)
            left = jnp.max(later[:, 0:LANES], axis=0, keepdims=True)
            left = jnp.max(jnp.where(col[0:1, 0:LANES] == 0, left, -jnp.inf))
            keys_remain = q0 + s * SQ + SQ - WIN > 0
            top = jnp.maximum(top, jnp.where(keys_remain, left, -jnp.inf))

    @pl.when(top > LOG_STICK_FLOOR)
    def _():
        tri = jnp.concatenate([tri_ref[0:TK, 0:TK], tri_ref[TK:2 * TK, 0:TK]], axis=1)
        n_blocks = (q0 + STEP_Q) // TK
        q_pos = q0 + lax.broadcasted_iota(jnp.int32, (2 * STEP_Q, TK), 0) % STEP_Q
        k_off = lax.broadcasted_iota(jnp.int32, (2 * STEP_Q, TK), 1)
        for p in range(pairs):
            lanes = slice(p * LANES, (p + 1) * LANES)
            q_st = _stack_heads(q_ref[:, lanes])
            acc_ref[...] = jnp.zeros_like(acc_ref)
            stick_ref[...] = jnp.zeros_like(stick_ref)

            def cond(carry):
                j, live = carry
                return jnp.logical_and(j < n_blocks, live)

            def body(carry):
                j, _ = carry
                ks = pl.multiple_of(q0 + STEP_Q - (j + 1) * TK, TK)
                z = _dot_nt(q_st, k_ref[pl.ds(ks, TK), lanes])
                z = jnp.where(ks + k_off < q_pos, z, -1e30)
                log_keep = _neg_softplus(z)
                hi, lo = _split_bf16(log_keep)
                sums = _dot(hi, tri) + _dot(lo, tri)
                stick = stick_ref[...]
                a = jnp.exp(z + log_keep + sums[:, :TK] + stick)
                acc_ref[...] += _dot(a.astype(BF16), v_ref[pl.ds(ks, TK), lanes])
                stick = stick + sums[:, TK:]
                stick_ref[...] = stick
                return j + 1, jnp.max(stick) > LOG_STICK_FLOOR

            lax.while_loop(cond, body, (jnp.int32(0), jnp.bool_(True)))
            o_ref[:, lanes] = _unstack_heads(acc_ref[...]).astype(BF16)


def _sb_attention(qkv, bsz, seq, sbw):
    n = qkv.shape[0]
    pairs = sbw // LANES
    steps = seq // STEP_Q
    units = (STEP_Q // SQ) * pairs
    r = lax.broadcasted_iota(jnp.int32, (2 * WIN, WIN), 0) % WIN
    c = lax.broadcasted_iota(jnp.int32, (2 * WIN, WIN), 1)
    tri = jnp.where(r > c, 1.0, 0.0).astype(BF16)
    return pl.pallas_call(
        functools.partial(_sb_attn_kernel, pairs=pairs),
        grid=(bsz, steps),
        in_specs=[
            pl.BlockSpec((STEP_Q, sbw), lambda b, i: (b * steps + i, 0)),
            pl.BlockSpec((seq, sbw), lambda b, i: (b, 1)),
            pl.BlockSpec((seq, sbw), lambda b, i: (b, 2)),
            _const_spec((2 * WIN, WIN)),
        ],
        out_specs=pl.BlockSpec((STEP_Q, sbw), lambda b, i: (b * steps + i, 0)),
        out_shape=jax.ShapeDtypeStruct((n, sbw), BF16),
        scratch_shapes=[
            pltpu.VMEM((units * 2 * SQ, 2 * WIN), BF16),
            pltpu.VMEM((units * 2 * SQ, WIN), F32),
            pltpu.VMEM((units * 2 * SQ, WIN), F32),
            pltpu.VMEM((STEP_Q // SQ, 2 * SQ, WIN), F32),
            pltpu.VMEM((2 * STEP_Q, LANES), F32),
            pltpu.VMEM((2 * STEP_Q, LANES), F32),
        ],
        compiler_params=_params(2),
        name="sb_attn",
    )(qkv, qkv, qkv, tri)


RUN = TM // SUBLANES
PITCH = RUN + SUBLANES
STAGE_BASE = CONV_HALO + SUBLANES
STAGE_ROWS = STAGE_BASE + (SUBLANES - 1) * PITCH + RUN
CONV_GROUPS = (CONV_K - 1) + RUN
CONV_BLOCK = 16
MIX_NORM_PARTS = 2


def _mix_out_kernel(u_ref, uh_ref, a_ref, x_ref, cw_ref, cb_ref, cg_ref, cbeta_ref, unperm_ref,
                    wo_ref, g_ref, b_ref, o_ref, stage_ref, il_ref, y_ref, h_ref,
                    *, tiles, tiles_per_seq, alpha, cch):
    _init_previous_tile(h_ref)
    at_start = (jnp.minimum(pl.program_id(0), tiles - 1) % tiles_per_seq) == 0
    hist = CONV_K - 1
    for s in range(cch // LANES):
        lanes = slice(s * LANES, (s + 1) * LANES)
        stage_ref[s, 0:CONV_HALO, :] = jnp.where(at_start, 0.0, uh_ref[:, lanes])
        for j in range(SUBLANES):
            r0 = STAGE_BASE + j * PITCH
            stage_ref[s, r0:r0 + RUN, :] = u_ref[j * RUN:(j + 1) * RUN, lanes]
        for g in range(CONV_GROUPS):
            off = g - hist
            start = STAGE_BASE + off - (0 if off >= 0 else PITCH - RUN)
            il_ref[s, g * SUBLANES:(g + 1) * SUBLANES, :] = stage_ref[
                s, pl.ds(start, SUBLANES, stride=PITCH), :]

    blocks = RUN // CONV_BLOCK

    def conv_block(idx, carry):
        s = idx // blocks
        base = pl.multiple_of((idx % blocks) * (CONV_BLOCK * SUBLANES), CONV_BLOCK * SUBLANES)
        taps = [jnp.broadcast_to(cw_ref[s, k:k + 1, :], (SUBLANES, LANES)) for k in range(CONV_K)]
        ys = [jnp.broadcast_to(cb_ref[s], (SUBLANES, LANES))] * CONV_BLOCK
        for g in range(CONV_BLOCK + hist):
            v = il_ref[s, pl.ds(base + g * SUBLANES, SUBLANES), :]
            for i in range(CONV_BLOCK):
                if 0 <= g - i < CONV_K:
                    ys[i] = ys[i] + taps[g - i] * v
        for i in range(CONV_BLOCK):
            y_ref[s, pl.ds(base + i * SUBLANES, SUBLANES), :] = ys[i]
        return carry

    lax.fori_loop(0, (cch // LANES) * blocks, conv_block, 0)
    slot, zeros = _norm_previous_tile(h_ref, g_ref, b_ref, o_ref, MIX_NORM_PARTS)
    y = jnp.concatenate([y_ref[s] for s in range(cch // LANES)], axis=1)
    yn = _layer_norm(y, cg_ref[...], cbeta_ref[...])
    act = (yn * jax.nn.sigmoid(yn)).astype(BF16)
    act = _dot(unperm_ref[...], act).astype(BF16)
    part_rows = TM // MIX_NORM_PARTS
    for k in range(MIX_NORM_PARTS):
        rows = slice(k * part_rows, (k + 1) * part_rows)
        mix = _dot(act[rows], wo_ref[0:cch, :]) + _dot(a_ref[rows, :], wo_ref[cch:, :])
        h_ref[slot, rows, :] = (alpha + zeros[k]) * x_ref[rows, :] + mix


def _lane_slabs(t):
    depth, r, c = t.shape
    return t.reshape(depth, r, c // LANES, LANES).transpose(0, 2, 1, 3)


def _unpermute_matrix():
    t = lax.broadcasted_iota(jnp.int32, (TM, TM), 0)
    c = lax.broadcasted_iota(jnp.int32, (TM, TM), 1)
    return jnp.where(c == SUBLANES * (t % RUN) + t // RUN, 1.0, 0.0).astype(BF16)


def _mix_out(u, a, x2, layer, cw_slabs, cb_slabs, cg, cbeta, unperm, w_out, g, b, seq, alpha):
    n, d = x2.shape
    cch = u.shape[1]
    tiles_per_seq = seq // TM
    halo_blocks = TM // CONV_HALO
    slabs = cch // LANES
    tiles = n // TM
    cur = lambda i: jnp.minimum(i, tiles - 1)
    return pl.pallas_call(
        functools.partial(_mix_out_kernel, tiles=tiles, tiles_per_seq=tiles_per_seq, alpha=alpha,
                          cch=cch),
        grid=(tiles + 1,),
        in_specs=[
            pl.BlockSpec((TM, cch), lambda i: (cur(i), 0)),
            pl.BlockSpec((CONV_HALO, cch),
                         lambda i: (jnp.maximum(cur(i) * halo_blocks - 1, 0), 0)),
            pl.BlockSpec((TM, a.shape[1]), lambda i: (cur(i), 0)),
            pl.BlockSpec((TM, d), lambda i: (cur(i), 0)),
            _layer_spec(cw_slabs, layer), _layer_spec(cb_slabs, layer),
            _layer_spec(cg, layer), _layer_spec(cbeta, layer),
            _const_spec((TM, TM)),
            _layer_spec(w_out, layer),
            _layer_spec(g, layer), _layer_spec(b, layer),
        ],
        out_specs=pl.BlockSpec((TM, d), lambda i: (jnp.maximum(i - 1, 0), 0)),
        out_shape=jax.ShapeDtypeStruct((n, d), F32),
        scratch_shapes=[
            pltpu.VMEM((slabs, STAGE_ROWS, LANES), F32),
            pltpu.VMEM((slabs, CONV_GROUPS * SUBLANES, LANES), F32),
            pltpu.VMEM((slabs, TM, LANES), F32),
            pltpu.VMEM((2, TM, d), F32),
        ],
        compiler_params=_params(1),
        name="mix_out",
    )(u, u, a, x2, cw_slabs, cb_slabs, cg, cbeta, unperm, w_out, g, b)


def _init_previous_tile(h_ref):
    @pl.when(pl.program_id(0) == 0)
    def _():
        h_ref[1] = jnp.zeros(h_ref.shape[1:], h_ref.dtype)


def _norm_previous_tile(h_ref, g_ref, b_ref, o_ref, parts):
    i = pl.program_id(0)
    rows = h_ref.shape[1] // parts
    zeros = []
    for k in range(parts):
        out = _layer_norm(h_ref[(i + 1) % 2, k * rows:(k + 1) * rows, :], g_ref[...], b_ref[...])
        o_ref[k * rows:(k + 1) * rows, :] = out
        zeros.append(jnp.sum(out, keepdims=True) * 0.0)
    return i % 2, zeros


def _cross_kernel(x_ref, wq_ref, k_ref, v_ref, wo_ref, g_ref, b_ref, o_ref, *, alpha, dh):
    part_rows = x_ref.shape[0] // CROSS_PARTS
    zero = None
    for k in range(CROSS_PARTS):
        rows = slice(k * part_rows, (k + 1) * part_rows)
        x = x_ref[rows, :]
        q = (_dot(x.astype(BF16), wq_ref[...]) * (dh ** -0.5)).astype(BF16)
        outs = []
        for h in range(MEM_HEADS):
            sl = slice(h * dh, (h + 1) * dh)
            s = _dot_nt(q[:, sl], k_ref[:, sl])
            top = jnp.max(s, axis=-1, keepdims=True)
            if zero is not None and h == MEM_HEADS - 1:
                top = top + zero
            e = jnp.exp(s - top)
            p = e / jnp.sum(e, axis=-1, keepdims=True)
            outs.append(_dot(p.astype(BF16), v_ref[:, sl]).astype(BF16))
        cross = _dot(jnp.concatenate(outs, axis=-1), wo_ref[...])
        out = _layer_norm(alpha * x + cross, g_ref[...], b_ref[...])
        o_ref[rows, :] = out
        zero = jnp.sum(out, keepdims=True) * 0.0


def _cross(x2, layer, wq, k, v, wo, g, b, seq, mem_len, alpha):
    n, d = x2.shape
    tm = TM * CROSS_PARTS
    assert seq % tm == 0
    tiles_per_seq = seq // tm
    return pl.pallas_call(
        functools.partial(_cross_kernel, alpha=alpha, dh=d // MEM_HEADS),
        grid=(n // tm,),
        in_specs=[
            pl.BlockSpec((tm, d), lambda i: (i, 0)),
            _layer_spec(wq, layer),
            pl.BlockSpec((None, mem_len, d), lambda i: (layer, i // tiles_per_seq, 0)),
            pl.BlockSpec((None, mem_len, d), lambda i: (layer, i // tiles_per_seq, 0)),
            _layer_spec(wo, layer),
            _layer_spec(g, layer), _layer_spec(b, layer),
        ],
        out_specs=pl.BlockSpec((tm, d), lambda i: (i, 0)),
        out_shape=jax.ShapeDtypeStruct((n, d), F32),
        compiler_params=_params(1),
        name="cross_attn",
    )(x2, wq, k, v, wo, g, b)


def _ffn_kernel(x_ref, xh_ref, wup_ref, cw_ref, cb_ref, wdn_ref, g_ref, b_ref, o_ref,
                xe_ref, up_ref, acc_ref, h_ref, *, tiles, tiles_per_seq, alpha, n_chunks):
    _init_previous_tile(h_ref)
    slot, zeros = _norm_previous_tile(h_ref, g_ref, b_ref, o_ref, FFN_NORM_PARTS)
    at_start = (jnp.minimum(pl.program_id(0), tiles - 1) % tiles_per_seq) == 0
    xe_ref[0:FFN_HALO] = jnp.where(at_start, 0.0, xh_ref[...]).astype(BF16)
    xe_ref[FFN_HALO:FFN_HALO + TM] = x_ref[...].astype(BF16)

    d_ff = n_chunks * FC
    val_cols = lambda c: slice(c * FC, (c + 1) * FC)
    gate_cols = lambda c: slice(d_ff + c * FC, d_ff + (c + 1) * FC)

    def project_up(c):
        xe = xe_ref[...]
        up_ref[c % 2, 0] = _dot(xe, wup_ref[:, val_cols(c)])
        up_ref[c % 2, 1] = _dot(xe, wup_ref[:, gate_cols(c)])

    def conv(e_ref, cols, bias):
        base = FFN_HALO - (FFN_K - 1)
        y = bias
        for k in range(FFN_K):
            y = y + cw_ref[k:k + 1, cols] * e_ref[base + k:base + k + TM, :]
        return y

    def hidden(c):
        bias = cb_ref[:, val_cols(c)]
        if c < FFN_NORM_PARTS:
            bias = bias + zeros[c]
        val = conv(up_ref.at[c % 2, 0], val_cols(c), bias)
        gate = conv(up_ref.at[c % 2, 1], gate_cols(c), cb_ref[:, gate_cols(c)])
        return ((gate * jax.nn.sigmoid(gate)) * val).astype(BF16)

    project_up(0)
    hids = []
    for c in range(n_chunks):
        if c + 1 < n_chunks:
            project_up(c + 1)
        hids.append(hidden(c))
        if len(hids) == 2 or c + 1 == n_chunks:
            first = c + 1 - len(hids)
            hid = hids[0] if len(hids) == 1 else jnp.concatenate(hids, axis=1)
            part = _dot(hid, wdn_ref[first * FC:(c + 1) * FC, :])
            if first == 0:
                acc_ref[...] = part
            else:
                acc_ref[...] += part
            hids = []
    h_ref[slot] = alpha * x_ref[...] + acc_ref[...]


def _ffn(x2, layer, wup, cw, cb, wdn, g, b, seq, alpha):
    n, d = x2.shape
    n_chunks = wdn.shape[1] // FC
    tiles_per_seq = seq // TM
    halo_blocks = TM // FFN_HALO
    tiles = n // TM
    cur = lambda i: jnp.minimum(i, tiles - 1)
    return pl.pallas_call(
        functools.partial(_ffn_kernel, tiles=tiles, tiles_per_seq=tiles_per_seq, alpha=alpha,
                          n_chunks=n_chunks),
        grid=(tiles + 1,),
        in_specs=[
            pl.BlockSpec((TM, d), lambda i: (cur(i), 0)),
            pl.BlockSpec((FFN_HALO, d), lambda i: (jnp.maximum(cur(i) * halo_blocks - 1, 0), 0)),
            _layer_spec(wup, layer),
            _layer_spec(cw, layer),
            _layer_spec(cb, layer),
            _layer_spec(wdn, layer),
            _layer_spec(g, layer), _layer_spec(b, layer),
        ],
        out_specs=pl.BlockSpec((TM, d), lambda i: (jnp.maximum(i - 1, 0), 0)),
        out_shape=jax.ShapeDtypeStruct((n, d), F32),
        scratch_shapes=[
            pltpu.VMEM((FFN_HALO + TM, d), BF16),
            pltpu.VMEM((2, 2, FFN_HALO + TM, FC), F32),
            pltpu.VMEM((TM, d), F32),
            pltpu.VMEM((2, TM, d), F32),
        ],
        compiler_params=_params(1),
        name="conv_ffn",
    )(x2, x2, wup, cw, cb, wdn, g, b)


def kernel(x, mem, w_in, conv_w, conv_b, conv_ln_g, conv_ln_b, w_out, ln1_g, ln1_b,
           mem_wq, mem_wk, mem_wv, mem_wo, ln2_g, ln2_b,
           ffn_up, ffn_conv_w, ffn_conv_b, ffn_down, ln3_g, ln3_b):
    bsz, seq, d = x.shape
    depth = w_in.shape[0]
    mem_len = mem.shape[1]
    cch = conv_w.shape[2]
    sbw = SB_HEADS * SB_HEAD_DIM
    d_ff = ffn_down.shape[1]
    alpha = (2.0 * depth) ** 0.25
    assert seq % TM == 0 and seq % STEP_Q == 0 and d_ff % FC == 0
    assert w_in.shape[2] == 2 * cch + 3 * sbw and sbw % LANES == 0

    w_in_b = w_in.astype(BF16)
    w_out_b = w_out.astype(BF16)
    wq_b = mem_wq.astype(BF16)
    wo_b = mem_wo.astype(BF16)
    wup_b = ffn_up.astype(BF16)
    wdn_b = ffn_down.astype(BF16)
    cw_slabs = _lane_slabs(conv_w)
    cb_slabs = _lane_slabs(_rows(conv_b))
    unperm = _unpermute_matrix()

    mem_k, mem_v = _memkv(mem.reshape(bsz * mem_len, d).astype(BF16),
                          mem_wk.astype(BF16), mem_wv.astype(BF16))

    h = x.reshape(bsz * seq, d)
    for l in range(depth):
        u, qkv = _inproj(h, w_in_b, l, cch, sbw)
        a = _sb_attention(qkv, bsz, seq, sbw)
        h = _mix_out(u, a, h, l, cw_slabs, cb_slabs, _rows(conv_ln_g), _rows(conv_ln_b), unperm,
                     w_out_b, _rows(ln1_g), _rows(ln1_b), seq, alpha)
        h = _cross(h, l, wq_b, mem_k, mem_v, wo_b, _rows(ln2_g), _rows(ln2_b),
                   seq, mem_len, alpha)
        h = _ffn(h, l, wup_b, ffn_conv_w, _rows(ffn_conv_b), wdn_b, _rows(ln3_g), _rows(ln3_b),
                 seq, alpha)
    return h.reshape(bsz, seq, d)
```

```python
import functools

import jax
import jax.numpy as jnp
from jax import lax
from jax.experimental import pallas as pl
from jax.experimental.pallas import tpu as pltpu

F32 = jnp.float32
BF16 = jnp.bfloat16

CONV_K = 31
SB_HEADS = 8
SB_HEAD_DIM = 64
MEM_HEADS = 4
FFN_K = 3
LN_EPS = 1e-5

LANES = 128
SUBLANES = 8
V7X_VMEM_BYTES = 64 * 1024 * 1024
VMEM_LIMIT = V7X_VMEM_BYTES - V7X_VMEM_BYTES // 8

TM = 512
TM_IN = 1024
CROSS_PARTS = 4
STEP_Q = 1024
SQ = 64
WIN = 256
TK = 128
FC = 256
CONV_HALO = 32
FFN_HALO = 8
FFN_NORM_PARTS = 8
LOG_STICK_FLOOR = -100.0
MASKED_LOGIT = -1e30


def _dot(a, b):
    return jnp.dot(a, b, preferred_element_type=F32)


def _dot_nt(a, b):
    return lax.dot_general(a, b, (((1,), (1,)), ((), ())), preferred_element_type=F32)


def _layer_norm(h, g, b):
    mu = jnp.mean(h, axis=-1, keepdims=True)
    d = h - mu
    var = jnp.mean(d * d, axis=-1, keepdims=True)
    return d * lax.rsqrt(var + LN_EPS) * g + b


def _exact_zero(t):
    return jnp.sum(t, keepdims=True) * 0.0


def _init_previous_tile(h_ref):
    @pl.when(pl.program_id(0) == 0)
    def _():
        h_ref[1] = jnp.zeros(h_ref.shape[1:], h_ref.dtype)


def _norm_previous_tile(h_ref, g_ref, b_ref, o_ref, parts):
    i = pl.program_id(0)
    rows = h_ref.shape[1] // parts
    zeros = []
    for k in range(parts):
        out = _layer_norm(h_ref[(i + 1) % 2, k * rows:(k + 1) * rows, :], g_ref[...], b_ref[...])
        o_ref[k * rows:(k + 1) * rows, :] = out
        zeros.append(_exact_zero(out))
    return i % 2, zeros


def _params(n_axes):
    return pltpu.CompilerParams(
        dimension_semantics=("arbitrary",) * n_axes, vmem_limit_bytes=VMEM_LIMIT)


def _const_spec(shape):
    nd = len(shape)
    return pl.BlockSpec(shape, lambda *_: (0,) * nd, pipeline_mode=pl.Buffered(1))


def _layer_spec(stacked, layer):
    tail = stacked.shape[1:]
    return pl.BlockSpec((None,) + tail, lambda *_: (layer,) + (0,) * len(tail),
                        pipeline_mode=pl.Buffered(1))


def _rows(stacked):
    return stacked[:, None, :]


def _memkv_kernel(mem_ref, wk_ref, wv_ref, k_ref, v_ref):
    m = mem_ref[...]
    k_ref[0] = _dot(m, wk_ref[0]).astype(BF16)
    v_ref[0] = _dot(m, wv_ref[0]).astype(BF16)


def _memkv(mem2, wk, wv):
    depth, d, _ = wk.shape
    rows = mem2.shape[0]
    return pl.pallas_call(
        _memkv_kernel,
        grid=(depth,),
        in_specs=[
            _const_spec((rows, d)),
            pl.BlockSpec((1, d, d), lambda l: (l, 0, 0)),
            pl.BlockSpec((1, d, d), lambda l: (l, 0, 0)),
        ],
        out_specs=[
            pl.BlockSpec((1, rows, d), lambda l: (l, 0, 0)),
            pl.BlockSpec((1, rows, d), lambda l: (l, 0, 0)),
        ],
        out_shape=[jax.ShapeDtypeStruct((depth, rows, d), BF16)] * 2,
        compiler_params=_params(1),
        name="mem_kv",
    )(mem2, wk, wv)


def _inproj_kernel(x_ref, w_ref, u_ref, qkv_ref, *, cch, sbw):
    xb = x_ref[...].astype(BF16)
    ga = _dot(xb, w_ref[:, 0:cch])
    gg = _dot(xb, w_ref[:, cch:2 * cch])
    u_ref[...] = ga * jax.nn.sigmoid(gg)
    q = _dot(xb, w_ref[:, 2 * cch:2 * cch + sbw]) * (SB_HEAD_DIM ** -0.5)
    qkv_ref[:, 0:sbw] = q.astype(BF16)
    qkv_ref[:, sbw:3 * sbw] = _dot(xb, w_ref[:, 2 * cch + sbw:]).astype(BF16)


def _inproj(x2, w_in, layer, cch, sbw):
    n, d = x2.shape
    return pl.pallas_call(
        functools.partial(_inproj_kernel, cch=cch, sbw=sbw),
        grid=(n // TM_IN,),
        in_specs=[
            pl.BlockSpec((TM_IN, d), lambda i: (i, 0)),
            _layer_spec(w_in, layer),
        ],
        out_specs=[
            pl.BlockSpec((TM_IN, cch), lambda i: (i, 0)),
            pl.BlockSpec((TM_IN, 3 * sbw), lambda i: (i, 0)),
        ],
        out_shape=[
            jax.ShapeDtypeStruct((n, cch), F32),
            jax.ShapeDtypeStruct((n, 3 * sbw), BF16),
        ],
        compiler_params=_params(1),
        name="in_proj",
    )(x2, w_in)


def _neg_softplus(z):
    return -(jnp.maximum(z, 0.0) + jnp.log(1.0 + jnp.exp(-jnp.abs(z))))


def _split_bf16(t):
    hi = t.astype(BF16)
    return hi, (t - hi.astype(F32)).astype(BF16)


def _stack_heads(q_pair):
    lane = lax.broadcasted_iota(jnp.int32, q_pair.shape, 1)
    zero = jnp.zeros_like(q_pair)
    return jnp.concatenate([jnp.where(lane < SB_HEAD_DIM, q_pair, zero),
                            jnp.where(lane < SB_HEAD_DIM, zero, q_pair)], axis=0)


def _unstack_heads(o_stacked):
    rows = o_stacked.shape[0] // 2
    lane = lax.broadcasted_iota(jnp.int32, (rows, LANES), 1)
    return jnp.where(lane < SB_HEAD_DIM, o_stacked[:rows], o_stacked[rows:])


def _sb_attn_kernel(q_ref, k_ref, v_ref, tri_ref, o_ref,
                    hl_ref, zl_ref, s_ref, bias_ref, acc_ref, stick_ref, *, pairs):
    q0 = pl.program_id(1) * STEP_Q
    n_sub = STEP_Q // SQ
    rows = 2 * SQ
    col = lax.broadcasted_iota(jnp.int32, (rows, WIN), 1)
    q_off = lax.broadcasted_iota(jnp.int32, (rows, WIN), 0) % SQ

    win_start = []
    top = jnp.float32(-jnp.inf)
    for s in range(n_sub):
        qs0 = q0 + s * SQ
        ws = pl.multiple_of(jnp.maximum(qs0 + SQ - WIN, 0), SQ)
        win_start.append(ws)
        bias_ref[s] = jnp.where(ws + col < qs0 + q_off, 0.0, MASKED_LOGIT)
        for p in range(pairs):
            u = s * pairs + p
            lanes = slice(p * LANES, (p + 1) * LANES)
            q_st = _stack_heads(q_ref[s * SQ:(s + 1) * SQ, lanes])
            z = _dot_nt(q_st, k_ref[pl.ds(ws, WIN), lanes]) + bias_ref[s]
            log_keep = _neg_softplus(z)
            hi, lo = _split_bf16(log_keep)
            hl_ref[u * rows:(u + 1) * rows, 0:WIN] = hi
            hl_ref[u * rows:(u + 1) * rows, WIN:2 * WIN] = lo
            zl_ref[u * rows:(u + 1) * rows, :] = z + log_keep
    s_ref[...] = _dot(hl_ref[...], tri_ref[...])
    for s in range(n_sub):
        ws = win_start[s]
        for p in range(pairs):
            u = s * pairs + p
            lanes = slice(p * LANES, (p + 1) * LANES)
            later = s_ref[u * rows:(u + 1) * rows, :]
            a = jnp.exp(zl_ref[u * rows:(u + 1) * rows, :] + later).astype(BF16)
            o = _dot(a, v_ref[pl.ds(ws, WIN), lanes])
            o_ref[s * SQ:(s + 1) * SQ, lanes] = _unstack_heads(o).astype(BF16)
            left = jnp.max(later[:, 0:LANES], axis=0, keepdims=True)
            left = jnp.max(jnp.where(col[0:1, 0:LANES] == 0, left, -jnp.inf))
            keys_remain = q0 + s * SQ + SQ - WIN > 0
            top = jnp.maximum(top, jnp.where(keys_remain, left, -jnp.inf))

    @pl.when(top > LOG_STICK_FLOOR)
    def _():
        tri = jnp.concatenate([tri_ref[0:TK, 0:TK], tri_ref[TK:2 * TK, 0:TK]], axis=1)
        n_blocks = (q0 + STEP_Q) // TK
        q_pos = q0 + lax.broadcasted_iota(jnp.int32, (2 * STEP_Q, TK), 0) % STEP_Q
        k_off = lax.broadcasted_iota(jnp.int32, (2 * STEP_Q, TK), 1)
        for p in range(pairs):
            lanes = slice(p * LANES, (p + 1) * LANES)
            q_st = _stack_heads(q_ref[:, lanes])
            acc_ref[...] = jnp.zeros_like(acc_ref)
            stick_ref[...] = jnp.zeros_like(stick_ref)

            def cond(carry):
                j, live = carry
                return jnp.logical_and(j < n_blocks, live)

            def body(carry):
                j, _ = carry
                ks = pl.multiple_of(q0 + STEP_Q - (j + 1) * TK, TK)
                z = _dot_nt(q_st, k_ref[pl.ds(ks, TK), lanes])
                z = jnp.where(ks + k_off < q_pos, z, MASKED_LOGIT)
                log_keep = _neg_softplus(z)
                hi, lo = _split_bf16(log_keep)
                sums = _dot(hi, tri) + _dot(lo, tri)
                stick = stick_ref[...]
                a = jnp.exp(z + log_keep + sums[:, :TK] + stick)
                acc_ref[...] += _dot(a.astype(BF16), v_ref[pl.ds(ks, TK), lanes])
                stick = stick + sums[:, TK:]
                stick_ref[...] = stick
                return j + 1, jnp.max(stick) > LOG_STICK_FLOOR

            lax.while_loop(cond, body, (jnp.int32(0), jnp.bool_(True)))
            o_ref[:, lanes] = _unstack_heads(acc_ref[...]).astype(BF16)


def _sb_attention(qkv, bsz, seq, sbw):
    n = qkv.shape[0]
    pairs = sbw // LANES
    steps = seq // STEP_Q
    units = (STEP_Q // SQ) * pairs
    r = lax.broadcasted_iota(jnp.int32, (2 * WIN, WIN), 0) % WIN
    c = lax.broadcasted_iota(jnp.int32, (2 * WIN, WIN), 1)
    tri = jnp.where(r > c, 1.0, 0.0).astype(BF16)
    return pl.pallas_call(
        functools.partial(_sb_attn_kernel, pairs=pairs),
        grid=(bsz, steps),
        in_specs=[
            pl.BlockSpec((STEP_Q, sbw), lambda b, i: (b * steps + i, 0)),
            pl.BlockSpec((seq, sbw), lambda b, i: (b, 1)),
            pl.BlockSpec((seq, sbw), lambda b, i: (b, 2)),
            _const_spec((2 * WIN, WIN)),
        ],
        out_specs=pl.BlockSpec((STEP_Q, sbw), lambda b, i: (b * steps + i, 0)),
        out_shape=jax.ShapeDtypeStruct((n, sbw), BF16),
        scratch_shapes=[
            pltpu.VMEM((units * 2 * SQ, 2 * WIN), BF16),
            pltpu.VMEM((units * 2 * SQ, WIN), F32),
            pltpu.VMEM((units * 2 * SQ, WIN), F32),
            pltpu.VMEM((STEP_Q // SQ, 2 * SQ, WIN), F32),
            pltpu.VMEM((2 * STEP_Q, LANES), F32),
            pltpu.VMEM((2 * STEP_Q, LANES), F32),
        ],
        compiler_params=_params(2),
        name="sb_attn",
    )(qkv, qkv, qkv, tri)


RUN = TM // SUBLANES
PITCH = RUN + SUBLANES
STAGE_BASE = CONV_HALO + SUBLANES
STAGE_ROWS = STAGE_BASE + (SUBLANES - 1) * PITCH + RUN
CONV_GROUPS = (CONV_K - 1) + RUN
CONV_BLOCK = 16
MIX_NORM_PARTS = 2


def _mix_out_kernel(u_ref, uh_ref, a_ref, x_ref, cw_ref, cb_ref, cg_ref, cbeta_ref, unperm_ref,
                    wo_ref, g_ref, b_ref, o_ref, stage_ref, il_ref, y_ref, h_ref,
                    *, tiles, tiles_per_seq, alpha, cch):
    _init_previous_tile(h_ref)
    at_start = (jnp.minimum(pl.program_id(0), tiles - 1) % tiles_per_seq) == 0
    hist = CONV_K - 1
    for s in range(cch // LANES):
        lanes = slice(s * LANES, (s + 1) * LANES)
        stage_ref[s, 0:CONV_HALO, :] = jnp.where(at_start, 0.0, uh_ref[:, lanes])
        for j in range(SUBLANES):
            r0 = STAGE_BASE + j * PITCH
            stage_ref[s, r0:r0 + RUN, :] = u_ref[j * RUN:(j + 1) * RUN, lanes]
        for g in range(CONV_GROUPS):
            off = g - hist
            start = STAGE_BASE + off - (0 if off >= 0 else PITCH - RUN)
            il_ref[s, g * SUBLANES:(g + 1) * SUBLANES, :] = stage_ref[
                s, pl.ds(start, SUBLANES, stride=PITCH), :]

    blocks = RUN // CONV_BLOCK

    def conv_block(idx, carry):
        s = idx // blocks
        base = pl.multiple_of((idx % blocks) * (CONV_BLOCK * SUBLANES), CONV_BLOCK * SUBLANES)
        taps = [jnp.broadcast_to(cw_ref[s, k:k + 1, :], (SUBLANES, LANES)) for k in range(CONV_K)]
        ys = [jnp.broadcast_to(cb_ref[s], (SUBLANES, LANES))] * CONV_BLOCK
        for g in range(CONV_BLOCK + hist):
            v = il_ref[s, pl.ds(base + g * SUBLANES, SUBLANES), :]
            for i in range(CONV_BLOCK):
                if 0 <= g - i < CONV_K:
                    ys[i] = ys[i] + taps[g - i] * v
        for i in range(CONV_BLOCK):
            y_ref[s, pl.ds(base + i * SUBLANES, SUBLANES), :] = ys[i]
        return carry

    lax.fori_loop(0, (cch // LANES) * blocks, conv_block, 0)
    slot, zeros = _norm_previous_tile(h_ref, g_ref, b_ref, o_ref, MIX_NORM_PARTS)
    y = jnp.concatenate([y_ref[s] for s in range(cch // LANES)], axis=1)
    yn = _layer_norm(y, cg_ref[...], cbeta_ref[...])
    act = (yn * jax.nn.sigmoid(yn)).astype(BF16)
    act = _dot(unperm_ref[...], act).astype(BF16)
    part_rows = TM // MIX_NORM_PARTS
    for k in range(MIX_NORM_PARTS):
        rows = slice(k * part_rows, (k + 1) * part_rows)
        mix = _dot(act[rows], wo_ref[0:cch, :]) + _dot(a_ref[rows, :], wo_ref[cch:, :])
        h_ref[slot, rows, :] = (alpha + zeros[k]) * x_ref[rows, :] + mix


def _lane_slabs(t):
    depth, r, c = t.shape
    return t.reshape(depth, r, c // LANES, LANES).transpose(0, 2, 1, 3)


def _unpermute_matrix():
    t = lax.broadcasted_iota(jnp.int32, (TM, TM), 0)
    c = lax.broadcasted_iota(jnp.int32, (TM, TM), 1)
    return jnp.where(c == SUBLANES * (t % RUN) + t // RUN, 1.0, 0.0).astype(BF16)


def _mix_out(u, a, x2, layer, cw_slabs, cb_slabs, cg, cbeta, unperm, w_out, g, b, seq, alpha):
    n, d = x2.shape
    cch = u.shape[1]
    tiles_per_seq = seq // TM
    halo_blocks = TM // CONV_HALO
    slabs = cch // LANES
    tiles = n // TM
    cur = lambda i: jnp.minimum(i, tiles - 1)
    return pl.pallas_call(
        functools.partial(_mix_out_kernel, tiles=tiles, tiles_per_seq=tiles_per_seq, alpha=alpha,
                          cch=cch),
        grid=(tiles + 1,),
        in_specs=[
            pl.BlockSpec((TM, cch), lambda i: (cur(i), 0)),
            pl.BlockSpec((CONV_HALO, cch),
                         lambda i: (jnp.maximum(cur(i) * halo_blocks - 1, 0), 0)),
            pl.BlockSpec((TM, a.shape[1]), lambda i: (cur(i), 0)),
            pl.BlockSpec((TM, d), lambda i: (cur(i), 0)),
            _layer_spec(cw_slabs, layer), _layer_spec(cb_slabs, layer),
            _layer_spec(cg, layer), _layer_spec(cbeta, layer),
            _const_spec((TM, TM)),
            _layer_spec(w_out, layer),
            _layer_spec(g, layer), _layer_spec(b, layer),
        ],
        out_specs=pl.BlockSpec((TM, d), lambda i: (jnp.maximum(i - 1, 0), 0)),
        out_shape=jax.ShapeDtypeStruct((n, d), F32),
        scratch_shapes=[
            pltpu.VMEM((slabs, STAGE_ROWS, LANES), F32),
            pltpu.VMEM((slabs, CONV_GROUPS * SUBLANES, LANES), F32),
            pltpu.VMEM((slabs, TM, LANES), F32),
            pltpu.VMEM((2, TM, d), F32),
        ],
        compiler_params=_params(1),
        name="mix_out",
    )(u, u, a, x2, cw_slabs, cb_slabs, cg, cbeta, unperm, w_out, g, b)


def _cross_kernel(x_ref, wq_ref, k_ref, v_ref, wo_ref, g_ref, b_ref, o_ref, *, alpha, dh):
    part_rows = x_ref.shape[0] // CROSS_PARTS
    zero = None
    for k in range(CROSS_PARTS):
        rows = slice(k * part_rows, (k + 1) * part_rows)
        x = x_ref[rows, :]
        q = (_dot(x.astype(BF16), wq_ref[...]) * (dh ** -0.5)).astype(BF16)
        outs = []
        for h in range(MEM_HEADS):
            sl = slice(h * dh, (h + 1) * dh)
            s = _dot_nt(q[:, sl], k_ref[:, sl])
            top = jnp.max(s, axis=-1, keepdims=True)
            if zero is not None and h == MEM_HEADS - 1:
                top = top + zero
            e = jnp.exp(s - top)
            p = e / jnp.sum(e, axis=-1, keepdims=True)
            outs.append(_dot(p.astype(BF16), v_ref[:, sl]).astype(BF16))
        cross = _dot(jnp.concatenate(outs, axis=-1), wo_ref[...])
        out = _layer_norm(alpha * x + cross, g_ref[...], b_ref[...])
        o_ref[rows, :] = out
        zero = _exact_zero(out)


def _cross(x2, layer, wq, k, v, wo, g, b, seq, mem_len, alpha):
    n, d = x2.shape
    tm = TM * CROSS_PARTS
    assert seq % tm == 0
    tiles_per_seq = seq // tm
    return pl.pallas_call(
        functools.partial(_cross_kernel, alpha=alpha, dh=d // MEM_HEADS),
        grid=(n // tm,),
        in_specs=[
            pl.BlockSpec((tm, d), lambda i: (i, 0)),
            _layer_spec(wq, layer),
            pl.BlockSpec((None, mem_len, d), lambda i: (layer, i // tiles_per_seq, 0)),
            pl.BlockSpec((None, mem_len, d), lambda i: (layer, i // tiles_per_seq, 0)),
            _layer_spec(wo, layer),
            _layer_spec(g, layer), _layer_spec(b, layer),
        ],
        out_specs=pl.BlockSpec((tm, d), lambda i: (i, 0)),
        out_shape=jax.ShapeDtypeStruct((n, d), F32),
        compiler_params=_params(1),
        name="cross_attn",
    )(x2, wq, k, v, wo, g, b)


def _ffn_kernel(x_ref, xh_ref, wup_ref, cw_ref, cb_ref, wdn_ref, g_ref, b_ref, o_ref,
                xe_ref, up_ref, acc_ref, h_ref, *, tiles, tiles_per_seq, alpha, n_chunks):
    _init_previous_tile(h_ref)
    slot, zeros = _norm_previous_tile(h_ref, g_ref, b_ref, o_ref, FFN_NORM_PARTS)
    at_start = (jnp.minimum(pl.program_id(0), tiles - 1) % tiles_per_seq) == 0
    xe_ref[0:FFN_HALO] = jnp.where(at_start, 0.0, xh_ref[...]).astype(BF16)
    xe_ref[FFN_HALO:FFN_HALO + TM] = x_ref[...].astype(BF16)

    d_ff = n_chunks * FC
    val_cols = lambda c: slice(c * FC, (c + 1) * FC)
    gate_cols = lambda c: slice(d_ff + c * FC, d_ff + (c + 1) * FC)

    def project_up(c):
        xe = xe_ref[...]
        up_ref[c % 2, 0] = _dot(xe, wup_ref[:, val_cols(c)])
        up_ref[c % 2, 1] = _dot(xe, wup_ref[:, gate_cols(c)])

    def conv(e_ref, cols, bias):
        base = FFN_HALO - (FFN_K - 1)
        y = bias
        for k in range(FFN_K):
            y = y + cw_ref[k:k + 1, cols] * e_ref[base + k:base + k + TM, :]
        return y

    def hidden(c):
        bias = cb_ref[:, val_cols(c)]
        if c < FFN_NORM_PARTS:
            bias = bias + zeros[c]
        val = conv(up_ref.at[c % 2, 0], val_cols(c), bias)
        gate = conv(up_ref.at[c % 2, 1], gate_cols(c), cb_ref[:, gate_cols(c)])
        return ((gate * jax.nn.sigmoid(gate)) * val).astype(BF16)

    project_up(0)
    hids = []
    for c in range(n_chunks):
        if c + 1 < n_chunks:
            project_up(c + 1)
        hids.append(hidden(c))
        if len(hids) == 2 or c + 1 == n_chunks:
            first = c + 1 - len(hids)
            hid = hids[0] if len(hids) == 1 else jnp.concatenate(hids, axis=1)
            part = _dot(hid, wdn_ref[first * FC:(c + 1) * FC, :])
            if first == 0:
                acc_ref[...] = part
            else:
                acc_ref[...] += part
            hids = []
    h_ref[slot] = alpha * x_ref[...] + acc_ref[...]


def _ffn(x2, layer, wup, cw, cb, wdn, g, b, seq, alpha):
    n, d = x2.shape
    n_chunks = wdn.shape[1] // FC
    tiles_per_seq = seq // TM
    halo_blocks = TM // FFN_HALO
    tiles = n // TM
    cur = lambda i: jnp.minimum(i, tiles - 1)
    return pl.pallas_call(
        functools.partial(_ffn_kernel, tiles=tiles, tiles_per_seq=tiles_per_seq, alpha=alpha,
                          n_chunks=n_chunks),
        grid=(tiles + 1,),
        in_specs=[
            pl.BlockSpec((TM, d), lambda i: (cur(i), 0)),
            pl.BlockSpec((FFN_HALO, d), lambda i: (jnp.maximum(cur(i) * halo_blocks - 1, 0), 0)),
            _layer_spec(wup, layer),
            _layer_spec(cw, layer),
            _layer_spec(cb, layer),
            _layer_spec(wdn, layer),
            _layer_spec(g, layer), _layer_spec(b, layer),
        ],
        out_specs=pl.BlockSpec((TM, d), lambda i: (jnp.maximum(i - 1, 0), 0)),
        out_shape=jax.ShapeDtypeStruct((n, d), F32),
        scratch_shapes=[
            pltpu.VMEM((FFN_HALO + TM, d), BF16),
            pltpu.VMEM((2, 2, FFN_HALO + TM, FC), F32),
            pltpu.VMEM((TM, d), F32),
            pltpu.VMEM((2, TM, d), F32),
        ],
        compiler_params=_params(1),
        name="conv_ffn",
    )(x2, x2, wup, cw, cb, wdn, g, b)


def kernel(x, mem, w_in, conv_w, conv_b, conv_ln_g, conv_ln_b, w_out, ln1_g, ln1_b,
           mem_wq, mem_wk, mem_wv, mem_wo, ln2_g, ln2_b,
           ffn_up, ffn_conv_w, ffn_conv_b, ffn_down, ln3_g, ln3_b):
    bsz, seq, d = x.shape
    depth = w_in.shape[0]
    mem_len = mem.shape[1]
    cch = conv_w.shape[2]
    sbw = SB_HEADS * SB_HEAD_DIM
    d_ff = ffn_down.shape[1]
    alpha = (2.0 * depth) ** 0.25
    assert seq % TM == 0 and seq % STEP_Q == 0 and d_ff % FC == 0
    assert w_in.shape[2] == 2 * cch + 3 * sbw and sbw % LANES == 0

    w_in_b = w_in.astype(BF16)
    w_out_b = w_out.astype(BF16)
    wq_b = mem_wq.astype(BF16)
    wo_b = mem_wo.astype(BF16)
    wup_b = ffn_up.astype(BF16)
    wdn_b = ffn_down.astype(BF16)
    cw_slabs = _lane_slabs(conv_w)
    cb_slabs = _lane_slabs(_rows(conv_b))
    unperm = _unpermute_matrix()

    mem_k, mem_v = _memkv(mem.reshape(bsz * mem_len, d).astype(BF16),
                          mem_wk.astype(BF16), mem_wv.astype(BF16))

    h = x.reshape(bsz * seq, d)
    for l in range(depth):
        u, qkv = _inproj(h, w_in_b, l, cch, sbw)
        a = _sb_attention(qkv, bsz, seq, sbw)
        h = _mix_out(u, a, h, l, cw_slabs, cb_slabs, _rows(conv_ln_g), _rows(conv_ln_b), unperm,
                     w_out_b, _rows(ln1_g), _rows(ln1_b), seq, alpha)
        h = _cross(h, l, wq_b, mem_k, mem_v, wo_b, _rows(ln2_g), _rows(ln2_b),
                   seq, mem_len, alpha)
        h = _ffn(h, l, wup_b, ffn_conv_w, _rows(ffn_conv_b), wdn_b, _rows(ln3_g), _rows(ln3_b),
                 seq, alpha)
    return h.reshape(bsz, seq, d)
```

```python
import functools

import jax
import jax.numpy as jnp
from jax import lax
from jax.experimental import pallas as pl
from jax.experimental.pallas import tpu as pltpu

F32 = jnp.float32
BF16 = jnp.bfloat16

CONV_K = 31
SB_HEADS = 8
SB_HEAD_DIM = 64
MEM_HEADS = 4
FFN_K = 3
LN_EPS = 1e-5

LANES = 128
SUBLANES = 8
V7X_VMEM_BYTES = 64 * 1024 * 1024
VMEM_LIMIT = V7X_VMEM_BYTES - V7X_VMEM_BYTES // 8

TM = 512
TM_IN = 1024
CROSS_PARTS = 4
STEP_Q = 1024
SQ = 64
WIN = 256
TK = 128
FC = 256
CONV_HALO = 32
FFN_HALO = 8
FFN_NORM_PARTS = 8
FFN_UP_SLOTS = 11
LOG_STICK_FLOOR = -100.0
MASKED_LOGIT = -1e30


def _dot(a, b):
    return jnp.dot(a, b, preferred_element_type=F32)


def _dot_nt(a, b):
    return lax.dot_general(a, b, (((1,), (1,)), ((), ())), preferred_element_type=F32)


def _layer_norm(h, g, b):
    mu = jnp.mean(h, axis=-1, keepdims=True)
    d = h - mu
    var = jnp.mean(d * d, axis=-1, keepdims=True)
    return d * lax.rsqrt(var + LN_EPS) * g + b


def _exact_zero(t):
    return jnp.sum(t, keepdims=True) * 0.0


def _init_previous_tile(h_ref):
    @pl.when(pl.program_id(0) == 0)
    def _():
        h_ref[1] = jnp.zeros(h_ref.shape[1:], h_ref.dtype)


def _norm_previous_tile(h_ref, g_ref, b_ref, o_ref, parts):
    i = pl.program_id(0)
    rows = h_ref.shape[1] // parts
    zeros = []
    for k in range(parts):
        out = _layer_norm(h_ref[(i + 1) % 2, k * rows:(k + 1) * rows, :], g_ref[...], b_ref[...])
        o_ref[k * rows:(k + 1) * rows, :] = out
        zeros.append(_exact_zero(out))
    return i % 2, zeros


def _params(n_axes):
    return pltpu.CompilerParams(
        dimension_semantics=("arbitrary",) * n_axes, vmem_limit_bytes=VMEM_LIMIT)


def _const_spec(shape):
    nd = len(shape)
    return pl.BlockSpec(shape, lambda *_: (0,) * nd, pipeline_mode=pl.Buffered(1))


def _layer_spec(stacked, layer):
    tail = stacked.shape[1:]
    return pl.BlockSpec((None,) + tail, lambda *_: (layer,) + (0,) * len(tail),
                        pipeline_mode=pl.Buffered(1))


def _rows(stacked):
    return stacked[:, None, :]


def _memkv_kernel(mem_ref, wk_ref, wv_ref, k_ref, v_ref):
    m = mem_ref[...]
    k_ref[0] = _dot(m, wk_ref[0]).astype(BF16)
    v_ref[0] = _dot(m, wv_ref[0]).astype(BF16)


def _memkv(mem2, wk, wv):
    depth, d, _ = wk.shape
    rows = mem2.shape[0]
    return pl.pallas_call(
        _memkv_kernel,
        grid=(depth,),
        in_specs=[
            _const_spec((rows, d)),
            pl.BlockSpec((1, d, d), lambda l: (l, 0, 0)),
            pl.BlockSpec((1, d, d), lambda l: (l, 0, 0)),
        ],
        out_specs=[
            pl.BlockSpec((1, rows, d), lambda l: (l, 0, 0)),
            pl.BlockSpec((1, rows, d), lambda l: (l, 0, 0)),
        ],
        out_shape=[jax.ShapeDtypeStruct((depth, rows, d), BF16)] * 2,
        compiler_params=_params(1),
        name="mem_kv",
    )(mem2, wk, wv)


def _inproj_kernel(x_ref, w_ref, u_ref, qkv_ref, *, cch, sbw):
    xb = x_ref[...].astype(BF16)
    ga = _dot(xb, w_ref[:, 0:cch])
    gg = _dot(xb, w_ref[:, cch:2 * cch])
    u_ref[...] = ga * jax.nn.sigmoid(gg)
    q = _dot(xb, w_ref[:, 2 * cch:2 * cch + sbw]) * (SB_HEAD_DIM ** -0.5)
    qkv_ref[:, 0:sbw] = q.astype(BF16)
    qkv_ref[:, sbw:3 * sbw] = _dot(xb, w_ref[:, 2 * cch + sbw:]).astype(BF16)


def _inproj(x2, w_in, layer, cch, sbw):
    n, d = x2.shape
    return pl.pallas_call(
        functools.partial(_inproj_kernel, cch=cch, sbw=sbw),
        grid=(n // TM_IN,),
        in_specs=[
            pl.BlockSpec((TM_IN, d), lambda i: (i, 0)),
            _layer_spec(w_in, layer),
        ],
        out_specs=[
            pl.BlockSpec((TM_IN, cch), lambda i: (i, 0)),
            pl.BlockSpec((TM_IN, 3 * sbw), lambda i: (i, 0)),
        ],
        out_shape=[
            jax.ShapeDtypeStruct((n, cch), F32),
            jax.ShapeDtypeStruct((n, 3 * sbw), BF16),
        ],
        compiler_params=_params(1),
        name="in_proj",
    )(x2, w_in)


def _neg_softplus(z):
    return -(jnp.maximum(z, 0.0) + jnp.log(1.0 + jnp.exp(-jnp.abs(z))))


def _split_bf16(t):
    hi = t.astype(BF16)
    return hi, (t - hi.astype(F32)).astype(BF16)


def _stack_heads(q_pair):
    lane = lax.broadcasted_iota(jnp.int32, q_pair.shape, 1)
    zero = jnp.zeros_like(q_pair)
    return jnp.concatenate([jnp.where(lane < SB_HEAD_DIM, q_pair, zero),
                            jnp.where(lane < SB_HEAD_DIM, zero, q_pair)], axis=0)


def _unstack_heads(o_stacked):
    rows = o_stacked.shape[0] // 2
    lane = lax.broadcasted_iota(jnp.int32, (rows, LANES), 1)
    return jnp.where(lane < SB_HEAD_DIM, o_stacked[:rows], o_stacked[rows:])


def _sb_attn_kernel(q_ref, k_ref, v_ref, tri_ref, o_ref,
                    hl_ref, zl_ref, s_ref, bias_ref, acc_ref, stick_ref, *, pairs):
    q0 = pl.program_id(1) * STEP_Q
    n_sub = STEP_Q // SQ
    rows = 2 * SQ
    col = lax.broadcasted_iota(jnp.int32, (rows, WIN), 1)
    q_off = lax.broadcasted_iota(jnp.int32, (rows, WIN), 0) % SQ

    win_start = []
    top = jnp.float32(-jnp.inf)
    for s in range(n_sub):
        qs0 = q0 + s * SQ
        ws = pl.multiple_of(jnp.maximum(qs0 + SQ - WIN, 0), SQ)
        win_start.append(ws)
        bias_ref[s] = jnp.where(ws + col < qs0 + q_off, 0.0, MASKED_LOGIT)
        for p in range(pairs):
            u = s * pairs + p
            lanes = slice(p * LANES, (p + 1) * LANES)
            q_st = _stack_heads(q_ref[s * SQ:(s + 1) * SQ, lanes])
            z = _dot_nt(q_st, k_ref[pl.ds(ws, WIN), lanes]) + bias_ref[s]
            log_keep = _neg_softplus(z)
            hi, lo = _split_bf16(log_keep)
            hl_ref[u * rows:(u + 1) * rows, 0:WIN] = hi
            hl_ref[u * rows:(u + 1) * rows, WIN:2 * WIN] = lo
            zl_ref[u * rows:(u + 1) * rows, :] = z + log_keep
    s_ref[...] = _dot(hl_ref[...], tri_ref[...])
    for s in range(n_sub):
        ws = win_start[s]
        for p in range(pairs):
            u = s * pairs + p
            lanes = slice(p * LANES, (p + 1) * LANES)
            later = s_ref[u * rows:(u + 1) * rows, :]
            a = jnp.exp(zl_ref[u * rows:(u + 1) * rows, :] + later).astype(BF16)
            o = _dot(a, v_ref[pl.ds(ws, WIN), lanes])
            o_ref[s * SQ:(s + 1) * SQ, lanes] = _unstack_heads(o).astype(BF16)
            left = jnp.max(later[:, 0:LANES], axis=0, keepdims=True)
            left = jnp.max(jnp.where(col[0:1, 0:LANES] == 0, left, -jnp.inf))
            keys_remain = q0 + s * SQ + SQ - WIN > 0
            top = jnp.maximum(top, jnp.where(keys_remain, left, -jnp.inf))

    @pl.when(top > LOG_STICK_FLOOR)
    def _():
        tri = jnp.concatenate([tri_ref[0:TK, 0:TK], tri_ref[TK:2 * TK, 0:TK]], axis=1)
        n_blocks = (q0 + STEP_Q) // TK
        q_pos = q0 + lax.broadcasted_iota(jnp.int32, (2 * STEP_Q, TK), 0) % STEP_Q
        k_off = lax.broadcasted_iota(jnp.int32, (2 * STEP_Q, TK), 1)
        for p in range(pairs):
            lanes = slice(p * LANES, (p + 1) * LANES)
            q_st = _stack_heads(q_ref[:, lanes])
            acc_ref[...] = jnp.zeros_like(acc_ref)
            stick_ref[...] = jnp.zeros_like(stick_ref)

            def cond(carry):
                j, live = carry
                return jnp.logical_and(j < n_blocks, live)

            def body(carry):
                j, _ = carry
                ks = pl.multiple_of(q0 + STEP_Q - (j + 1) * TK, TK)
                z = _dot_nt(q_st, k_ref[pl.ds(ks, TK), lanes])
                z = jnp.where(ks + k_off < q_pos, z, MASKED_LOGIT)
                log_keep = _neg_softplus(z)
                hi, lo = _split_bf16(log_keep)
                sums = _dot(hi, tri) + _dot(lo, tri)
                stick = stick_ref[...]
                a = jnp.exp(z + log_keep + sums[:, :TK] + stick)
                acc_ref[...] += _dot(a.astype(BF16), v_ref[pl.ds(ks, TK), lanes])
                stick = stick + sums[:, TK:]
                stick_ref[...] = stick
                return j + 1, jnp.max(stick) > LOG_STICK_FLOOR

            lax.while_loop(cond, body, (jnp.int32(0), jnp.bool_(True)))
            o_ref[:, lanes] = _unstack_heads(acc_ref[...]).astype(BF16)


def _sb_attention(qkv, bsz, seq, sbw):
    n = qkv.shape[0]
    pairs = sbw // LANES
    steps = seq // STEP_Q
    units = (STEP_Q // SQ) * pairs
    r = lax.broadcasted_iota(jnp.int32, (2 * WIN, WIN), 0) % WIN
    c = lax.broadcasted_iota(jnp.int32, (2 * WIN, WIN), 1)
    tri = jnp.where(r > c, 1.0, 0.0).astype(BF16)
    return pl.pallas_call(
        functools.partial(_sb_attn_kernel, pairs=pairs),
        grid=(bsz, steps),
        in_specs=[
            pl.BlockSpec((STEP_Q, sbw), lambda b, i: (b * steps + i, 0)),
            pl.BlockSpec((seq, sbw), lambda b, i: (b, 1)),
            pl.BlockSpec((seq, sbw), lambda b, i: (b, 2)),
            _const_spec((2 * WIN, WIN)),
        ],
        out_specs=pl.BlockSpec((STEP_Q, sbw), lambda b, i: (b * steps + i, 0)),
        out_shape=jax.ShapeDtypeStruct((n, sbw), BF16),
        scratch_shapes=[
            pltpu.VMEM((units * 2 * SQ, 2 * WIN), BF16),
            pltpu.VMEM((units * 2 * SQ, WIN), F32),
            pltpu.VMEM((units * 2 * SQ, WIN), F32),
            pltpu.VMEM((STEP_Q // SQ, 2 * SQ, WIN), F32),
            pltpu.VMEM((2 * STEP_Q, LANES), F32),
            pltpu.VMEM((2 * STEP_Q, LANES), F32),
        ],
        compiler_params=_params(2),
        name="sb_attn",
    )(qkv, qkv, qkv, tri)


RUN = TM // SUBLANES
PITCH = RUN + SUBLANES
STAGE_BASE = CONV_HALO + SUBLANES
STAGE_ROWS = STAGE_BASE + (SUBLANES - 1) * PITCH + RUN
CONV_GROUPS = (CONV_K - 1) + RUN
CONV_BLOCK = 16
MIX_NORM_PARTS = 2


def _mix_out_kernel(u_ref, uh_ref, a_ref, x_ref, cw_ref, cb_ref, cg_ref, cbeta_ref, unperm_ref,
                    wo_ref, g_ref, b_ref, o_ref, stage_ref, il_ref, y_ref, h_ref,
                    *, tiles, tiles_per_seq, alpha, cch):
    _init_previous_tile(h_ref)
    at_start = (jnp.minimum(pl.program_id(0), tiles - 1) % tiles_per_seq) == 0
    hist = CONV_K - 1
    for s in range(cch // LANES):
        lanes = slice(s * LANES, (s + 1) * LANES)
        stage_ref[s, 0:CONV_HALO, :] = jnp.where(at_start, 0.0, uh_ref[:, lanes])
        for j in range(SUBLANES):
            r0 = STAGE_BASE + j * PITCH
            stage_ref[s, r0:r0 + RUN, :] = u_ref[j * RUN:(j + 1) * RUN, lanes]
        for g in range(CONV_GROUPS):
            off = g - hist
            start = STAGE_BASE + off - (0 if off >= 0 else PITCH - RUN)
            il_ref[s, g * SUBLANES:(g + 1) * SUBLANES, :] = stage_ref[
                s, pl.ds(start, SUBLANES, stride=PITCH), :]

    blocks = RUN // CONV_BLOCK

    def conv_block(idx, carry):
        s = idx // blocks
        base = pl.multiple_of((idx % blocks) * (CONV_BLOCK * SUBLANES), CONV_BLOCK * SUBLANES)
        taps = [jnp.broadcast_to(cw_ref[s, k:k + 1, :], (SUBLANES, LANES)) for k in range(CONV_K)]
        ys = [jnp.broadcast_to(cb_ref[s], (SUBLANES, LANES))] * CONV_BLOCK
        for g in range(CONV_BLOCK + hist):
            v = il_ref[s, pl.ds(base + g * SUBLANES, SUBLANES), :]
            for i in range(CONV_BLOCK):
                if 0 <= g - i < CONV_K:
                    ys[i] = ys[i] + taps[g - i] * v
        for i in range(CONV_BLOCK):
            y_ref[s, pl.ds(base + i * SUBLANES, SUBLANES), :] = ys[i]
        return carry

    lax.fori_loop(0, (cch // LANES) * blocks, conv_block, 0)
    slot, zeros = _norm_previous_tile(h_ref, g_ref, b_ref, o_ref, MIX_NORM_PARTS)
    y = jnp.concatenate([y_ref[s] for s in range(cch // LANES)], axis=1)
    yn = _layer_norm(y, cg_ref[...], cbeta_ref[...])
    act = (yn * jax.nn.sigmoid(yn)).astype(BF16)
    act = _dot(unperm_ref[...], act).astype(BF16)
    part_rows = TM // MIX_NORM_PARTS
    for k in range(MIX_NORM_PARTS):
        rows = slice(k * part_rows, (k + 1) * part_rows)
        mix = _dot(act[rows], wo_ref[0:cch, :]) + _dot(a_ref[rows, :], wo_ref[cch:, :])
        h_ref[slot, rows, :] = (alpha + zeros[k]) * x_ref[rows, :] + mix


def _lane_slabs(t):
    depth, r, c = t.shape
    return t.reshape(depth, r, c // LANES, LANES).transpose(0, 2, 1, 3)


def _unpermute_matrix():
    t = lax.broadcasted_iota(jnp.int32, (TM, TM), 0)
    c = lax.broadcasted_iota(jnp.int32, (TM, TM), 1)
    return jnp.where(c == SUBLANES * (t % RUN) + t // RUN, 1.0, 0.0).astype(BF16)


def _mix_out(u, a, x2, layer, cw_slabs, cb_slabs, cg, cbeta, unperm, w_out, g, b, seq, alpha):
    n, d = x2.shape
    cch = u.shape[1]
    tiles_per_seq = seq // TM
    halo_blocks = TM // CONV_HALO
    slabs = cch // LANES
    tiles = n // TM
    cur = lambda i: jnp.minimum(i, tiles - 1)
    return pl.pallas_call(
        functools.partial(_mix_out_kernel, tiles=tiles, tiles_per_seq=tiles_per_seq, alpha=alpha,
                          cch=cch),
        grid=(tiles + 1,),
        in_specs=[
            pl.BlockSpec((TM, cch), lambda i: (cur(i), 0)),
            pl.BlockSpec((CONV_HALO, cch),
                         lambda i: (jnp.maximum(cur(i) * halo_blocks - 1, 0), 0)),
            pl.BlockSpec((TM, a.shape[1]), lambda i: (cur(i), 0)),
            pl.BlockSpec((TM, d), lambda i: (cur(i), 0)),
            _layer_spec(cw_slabs, layer), _layer_spec(cb_slabs, layer),
            _layer_spec(cg, layer), _layer_spec(cbeta, layer),
            _const_spec((TM, TM)),
            _layer_spec(w_out, layer),
            _layer_spec(g, layer), _layer_spec(b, layer),
        ],
        out_specs=pl.BlockSpec((TM, d), lambda i: (jnp.maximum(i - 1, 0), 0)),
        out_shape=jax.ShapeDtypeStruct((n, d), F32),
        scratch_shapes=[
            pltpu.VMEM((slabs, STAGE_ROWS, LANES), F32),
            pltpu.VMEM((slabs, CONV_GROUPS * SUBLANES, LANES), F32),
            pltpu.VMEM((slabs, TM, LANES), F32),
            pltpu.VMEM((2, TM, d), F32),
        ],
        compiler_params=_params(1),
        name="mix_out",
    )(u, u, a, x2, cw_slabs, cb_slabs, cg, cbeta, unperm, w_out, g, b)


def _cross_kernel(x_ref, wq_ref, k_ref, v_ref, wo_ref, g_ref, b_ref, o_ref, *, alpha, dh):
    part_rows = x_ref.shape[0] // CROSS_PARTS
    zero = None
    for k in range(CROSS_PARTS):
        rows = slice(k * part_rows, (k + 1) * part_rows)
        x = x_ref[rows, :]
        q = (_dot(x.astype(BF16), wq_ref[...]) * (dh ** -0.5)).astype(BF16)
        outs = []
        for h in range(MEM_HEADS):
            sl = slice(h * dh, (h + 1) * dh)
            s = _dot_nt(q[:, sl], k_ref[:, sl])
            top = jnp.max(s, axis=-1, keepdims=True)
            if zero is not None and h == MEM_HEADS - 1:
                top = top + zero
            e = jnp.exp(s - top)
            p = e / jnp.sum(e, axis=-1, keepdims=True)
            outs.append(_dot(p.astype(BF16), v_ref[:, sl]).astype(BF16))
        cross = _dot(jnp.concatenate(outs, axis=-1), wo_ref[...])
        out = _layer_norm(alpha * x + cross, g_ref[...], b_ref[...])
        o_ref[rows, :] = out
        zero = _exact_zero(out)


def _cross(x2, layer, wq, k, v, wo, g, b, seq, mem_len, alpha):
    n, d = x2.shape
    tm = TM * CROSS_PARTS
    assert seq % tm == 0
    tiles_per_seq = seq // tm
    return pl.pallas_call(
        functools.partial(_cross_kernel, alpha=alpha, dh=d // MEM_HEADS),
        grid=(n // tm,),
        in_specs=[
            pl.BlockSpec((tm, d), lambda i: (i, 0)),
            _layer_spec(wq, layer),
            pl.BlockSpec((None, mem_len, d), lambda i: (layer, i // tiles_per_seq, 0)),
            pl.BlockSpec((None, mem_len, d), lambda i: (layer, i // tiles_per_seq, 0)),
            _layer_spec(wo, layer),
            _layer_spec(g, layer), _layer_spec(b, layer),
        ],
        out_specs=pl.BlockSpec((tm, d), lambda i: (i, 0)),
        out_shape=jax.ShapeDtypeStruct((n, d), F32),
        compiler_params=_params(1),
        name="cross_attn",
    )(x2, wq, k, v, wo, g, b)


def _ffn_kernel(x_ref, xh_ref, wup_ref, cw_ref, cb_ref, wdn_ref, g_ref, b_ref, o_ref,
                xe_ref, up_ref, acc_ref, h_ref, *, tiles, tiles_per_seq, alpha, n_chunks):
    _init_previous_tile(h_ref)
    slot, zeros = _norm_previous_tile(h_ref, g_ref, b_ref, o_ref, FFN_NORM_PARTS)
    at_start = (jnp.minimum(pl.program_id(0), tiles - 1) % tiles_per_seq) == 0
    xe_ref[0:FFN_HALO] = jnp.where(at_start, 0.0, xh_ref[...]).astype(BF16)
    xe_ref[FFN_HALO:FFN_HALO + TM] = x_ref[...].astype(BF16)

    d_ff = n_chunks * FC
    val_cols = lambda c: slice(c * FC, (c + 1) * FC)
    gate_cols = lambda c: slice(d_ff + c * FC, d_ff + (c + 1) * FC)

    def project_up(c):
        xe = xe_ref[...]
        up_ref[c % FFN_UP_SLOTS,0] = _dot(xe, wup_ref[:, val_cols(c)])
        up_ref[c % FFN_UP_SLOTS,1] = _dot(xe, wup_ref[:, gate_cols(c)])

    def conv(e_ref, cols, bias):
        base = FFN_HALO - (FFN_K - 1)
        y = bias
        for k in range(FFN_K):
            y = y + cw_ref[k:k + 1, cols] * e_ref[base + k:base + k + TM, :]
        return y

    def hidden(c):
        bias = cb_ref[:, val_cols(c)]
        if c < FFN_NORM_PARTS:
            bias = bias + zeros[c]
        val = conv(up_ref.at[c % FFN_UP_SLOTS,0], val_cols(c), bias)
        gate = conv(up_ref.at[c % FFN_UP_SLOTS,1], gate_cols(c), cb_ref[:, gate_cols(c)])
        return ((gate * jax.nn.sigmoid(gate)) * val).astype(BF16)

    ahead = FFN_UP_SLOTS - 1
    for c in range(ahead):
        project_up(c)
    hids = []
    for c in range(n_chunks):
        if c + ahead < n_chunks:
            project_up(c + ahead)
        hids.append(hidden(c))
        if len(hids) == 2 or c + 1 == n_chunks:
            first = c + 1 - len(hids)
            hid = hids[0] if len(hids) == 1 else jnp.concatenate(hids, axis=1)
            part = _dot(hid, wdn_ref[first * FC:(c + 1) * FC, :])
            if first == 0:
                acc_ref[...] = part
            else:
                acc_ref[...] += part
            hids = []
    h_ref[slot] = alpha * x_ref[...] + acc_ref[...]


def _ffn(x2, layer, wup, cw, cb, wdn, g, b, seq, alpha):
    n, d = x2.shape
    n_chunks = wdn.shape[1] // FC
    tiles_per_seq = seq // TM
    halo_blocks = TM // FFN_HALO
    tiles = n // TM
    cur = lambda i: jnp.minimum(i, tiles - 1)
    return pl.pallas_call(
        functools.partial(_ffn_kernel, tiles=tiles, tiles_per_seq=tiles_per_seq, alpha=alpha,
                          n_chunks=n_chunks),
        grid=(tiles + 1,),
        in_specs=[
            pl.BlockSpec((TM, d), lambda i: (cur(i), 0)),
            pl.BlockSpec((FFN_HALO, d), lambda i: (jnp.maximum(cur(i) * halo_blocks - 1, 0), 0)),
            _layer_spec(wup, layer),
            _layer_spec(cw, layer),
            _layer_spec(cb, layer),
            _layer_spec(wdn, layer),
            _layer_spec(g, layer), _layer_spec(b, layer),
        ],
        out_specs=pl.BlockSpec((TM, d), lambda i: (jnp.maximum(i - 1, 0), 0)),
        out_shape=jax.ShapeDtypeStruct((n, d), F32),
        scratch_shapes=[
            pltpu.VMEM((FFN_HALO + TM, d), BF16),
            pltpu.VMEM((FFN_UP_SLOTS, 2, FFN_HALO + TM, FC), F32),
            pltpu.VMEM((TM, d), F32),
            pltpu.VMEM((2, TM, d), F32),
        ],
        compiler_params=_params(1),
        name="conv_ffn",
    )(x2, x2, wup, cw, cb, wdn, g, b)


def kernel(x, mem, w_in, conv_w, conv_b, conv_ln_g, conv_ln_b, w_out, ln1_g, ln1_b,
           mem_wq, mem_wk, mem_wv, mem_wo, ln2_g, ln2_b,
           ffn_up, ffn_conv_w, ffn_conv_b, ffn_down, ln3_g, ln3_b):
    bsz, seq, d = x.shape
    depth = w_in.shape[0]
    mem_len = mem.shape[1]
    cch = conv_w.shape[2]
    sbw = SB_HEADS * SB_HEAD_DIM
    d_ff = ffn_down.shape[1]
    alpha = (2.0 * depth) ** 0.25
    assert seq % TM == 0 and seq % STEP_Q == 0 and d_ff % FC == 0
    assert w_in.shape[2] == 2 * cch + 3 * sbw and sbw % LANES == 0

    w_in_b = w_in.astype(BF16)
    w_out_b = w_out.astype(BF16)
    wq_b = mem_wq.astype(BF16)
    wo_b = mem_wo.astype(BF16)
    wup_b = ffn_up.astype(BF16)
    wdn_b = ffn_down.astype(BF16)
    cw_slabs = _lane_slabs(conv_w)
    cb_slabs = _lane_slabs(_rows(conv_b))
    unperm = _unpermute_matrix()

    mem_k, mem_v = _memkv(mem.reshape(bsz * mem_len, d).astype(BF16),
                          mem_wk.astype(BF16), mem_wv.astype(BF16))

    h = x.reshape(bsz * seq, d)
    for l in range(depth):
        u, qkv = _inproj(h, w_in_b, l, cch, sbw)
        a = _sb_attention(qkv, bsz, seq, sbw)
        h = _mix_out(u, a, h, l, cw_slabs, cb_slabs, _rows(conv_ln_g), _rows(conv_ln_b), unperm,
                     w_out_b, _rows(ln1_g), _rows(ln1_b), seq, alpha)
        h = _cross(h, l, wq_b, mem_k, mem_v, wo_b, _rows(ln2_g), _rows(ln2_b),
                   seq, mem_len, alpha)
        h = _ffn(h, l, wup_b, ffn_conv_w, _rows(ffn_conv_b), wdn_b, _rows(ln3_g), _rows(ln3_b),
                 seq, alpha)
    return h.reshape(bsz, seq, d)
```
